```python
import jax, jax.numpy as jnp
from jax import lax
import numpy as np

D_MODEL = 1024
BATCH = 32
SEQ = 256
DEPTH = 2
DEC_BATCH = 8
DEC_SEQ = 1024
PAST_LEN = 256

GRID_W = 64
POOL_WIDTH = 256
POOL_GROUPS = 4
POOL_GROUP_DIM = POOL_WIDTH // POOL_GROUPS
POOL_WINDOWS = (2, 4, 8, 16)
NA_HEADS = 8
HEAD_DIM = 64
NA_WIDTH = NA_HEADS * HEAD_DIM
NA_ROWS_MAX = 8
NA_COLS = 16
SG_WIDTH = 256
SG_GROUPS = 4
SG_GROUP_DIM = SG_WIDTH // SG_GROUPS
CHUNK = 128
D_FF = 2816
N_MOD = 9
MIX_WIDTH = POOL_WIDTH + NA_WIDTH + SG_WIDTH
IN_WIDTH = POOL_WIDTH + 3 * NA_WIDTH + 2 * SG_WIDTH
IN_SPLITS = (POOL_WIDTH, POOL_WIDTH + NA_WIDTH, POOL_WIDTH + 2 * NA_WIDTH,
             POOL_WIDTH + 3 * NA_WIDTH, POOL_WIDTH + 3 * NA_WIDTH + SG_WIDTH)
EPS = 1e-6

kernel_name = "hybrid_pool_natten_sgmlp_diffusion_step"


def _rms(x, g):
    x32 = x.astype(jnp.float32)
    y = x32 * lax.rsqrt(jnp.mean(x32 * x32, axis=-1, keepdims=True) + EPS)
    return y.astype(x.dtype) * g


def _modulate(h, shift, scale):
    return h * (1.0 + scale) + shift


def _swiglu(h, w_gate, w_up, w_down):
    return (jax.nn.silu(h @ w_gate) * (h @ w_up)) @ w_down


def _half_ffn(x, gain, shift, scale, gate, w_gate, w_up, w_down):
    return x + 0.5 * gate * _swiglu(_modulate(_rms(x, gain), shift, scale), w_gate, w_up, w_down)


def _heads(x):
    b, l, _ = x.shape
    return x.reshape(b, l, NA_HEADS, HEAD_DIM).transpose(0, 2, 1, 3)


def _pool_mixer(a, w_pool, pool_scale):
    b, l, _ = a.shape
    a32 = a.astype(jnp.float32)
    csum = jnp.concatenate([jnp.zeros_like(a32[:, :1]), jnp.cumsum(a32, axis=1)], axis=1)
    t = jnp.arange(l)
    outs = []
    for gi, w in enumerate(POOL_WINDOWS):
        lo = jnp.clip(t - w // 2, 0, l)
        hi = jnp.clip(t + w // 2, 0, l)
        cg = csum[..., gi * POOL_GROUP_DIM:(gi + 1) * POOL_GROUP_DIM]
        cnt = (hi - lo).astype(jnp.float32)[None, :, None]
        outs.append((cg[:, hi] - cg[:, lo]) / cnt)
    pooled = jnp.concatenate(outs, axis=-1).astype(a.dtype) - a
    pooled = pooled.reshape(b, l, POOL_GROUPS, POOL_GROUP_DIM)
    mixed = jnp.einsum('blgc,gcd->blgd', pooled, w_pool).reshape(b, l, POOL_WIDTH)
    return mixed * pool_scale


def _chunk_mixer(u, v, vnorm_g, w_s, b_s):
    b, l, _ = u.shape
    n = l // CHUNK
    u = jax.nn.gelu(u)
    vg = _rms(jax.nn.gelu(v).reshape(b, n, CHUNK, SG_GROUPS, SG_GROUP_DIM), vnorm_g)
    sp = jnp.einsum('gpq,bnqgc->bnpgc', w_s, vg) + b_s.T[:, :, None]
    return u * sp.reshape(b, l, SG_WIDTH)


def _context_attention(q, k, v):
    b, h, lc, d = q.shape
    nb = lc // CHUNK
    scale = d ** -0.5
    qb = q.reshape(b, h, nb, CHUNK, d).transpose(2, 0, 1, 3, 4)

    def block(qi):
        s = jnp.einsum('bhqd,bhkd->bhqk', qi, k).astype(jnp.float32) * scale
        p = jax.nn.softmax(s, axis=-1).astype(v.dtype)
        return jnp.einsum('bhqk,bhkd->bhqd', p, v)

    o = lax.map(block, qb)
    return o.transpose(1, 2, 0, 3, 4).reshape(b, h, lc, d)


def _neighbourhood_attention(q, k, v, k_ctx, v_ctx, rpb):
    b, h, l, d = q.shape
    rows = l // GRID_W
    kh = min(NA_ROWS_MAX, rows)
    scale = d ** -0.5
    kg = k.reshape(b, h, rows, GRID_W, d)
    vg = v.reshape(b, h, rows, GRID_W, d)
    qg = q.reshape(b, h, rows, GRID_W, d).transpose(2, 0, 1, 3, 4)
    cols = jnp.arange(GRID_W)
    col_start = jnp.clip(cols - NA_COLS // 2, 0, GRID_W - NA_COLS)
    col_ok = (cols[None, :] >= col_start[:, None]) & (cols[None, :] < col_start[:, None] + NA_COLS)
    dc_idx = jnp.clip(cols[None, :] - cols[:, None], -(NA_COLS - 1), NA_COLS - 1) + (NA_COLS - 1)

    def row_block(args):
        r, q_r = args
        start = jnp.clip(r - kh // 2, 0, rows - kh)
        k_r = lax.dynamic_slice_in_dim(kg, start, kh, axis=2)
        v_r = lax.dynamic_slice_in_dim(vg, start, kh, axis=2)
        dr_idx = start + jnp.arange(kh) - r + (NA_ROWS_MAX - 1)
        bias = rpb[:, dr_idx][:, :, dc_idx].transpose(0, 2, 1, 3)
        s_loc = jnp.einsum('bhqd,bhkwd->bhqkw', q_r, k_r).astype(jnp.float32) * scale
        s_loc = jnp.where(col_ok[:, None, :], s_loc + bias[None].astype(jnp.float32), -jnp.inf)
        s_ctx = jnp.einsum('bhqd,bhcd->bhqc', q_r, k_ctx).astype(jnp.float32) * scale
        s = jnp.concatenate([s_loc.reshape(b, h, GRID_W, kh * GRID_W), s_ctx], axis=-1)
        p = jax.nn.softmax(s, axis=-1).astype(v.dtype)
        p_loc = p[..., :kh * GRID_W].reshape(b, h, GRID_W, kh, GRID_W)
        p_ctx = p[..., kh * GRID_W:]
        return (jnp.einsum('bhqkw,bhkwd->bhqd', p_loc, v_r)
                + jnp.einsum('bhqc,bhcd->bhqd', p_ctx, v_ctx))

    o = lax.map(row_block, (jnp.arange(rows), qg))
    return o.transpose(1, 2, 0, 3, 4).reshape(b, h, l, d)


def _mixer_inputs(h, w_in, q_norm_g, k_norm_g):
    z = h @ w_in
    za, zq, zk, zv, zu, zsv = jnp.split(z, IN_SPLITS, axis=-1)
    q = _rms(_heads(zq), q_norm_g)
    k = _rms(_heads(zk), k_norm_g)
    return za, q, k, _heads(zv), zu, zsv


def _mixer_output(a_out, attn_out, sg_out, out_norm_g, w_out):
    b, h, l, d = attn_out.shape
    o_b = attn_out.transpose(0, 2, 1, 3).reshape(b, l, NA_WIDTH)
    g_a, g_b, g_c = jnp.split(out_norm_g, [POOL_WIDTH, POOL_WIDTH + NA_WIDTH])
    o = jnp.concatenate([_rms(a_out, g_a), _rms(o_b, g_b), _rms(sg_out, g_c)], axis=-1)
    return o @ w_out


def setup_inputs(seed: int = 0) -> dict:
    key = jax.random.key(seed)
    ks = jax.random.split(key, 24)

    def nrm(k, shape, scale):
        return jax.random.normal(k, shape, jnp.float32) * scale

    return {
        "x_prompt": nrm(ks[0], (BATCH, SEQ, D_MODEL), 1.0),
        "x_sample": nrm(ks[1], (DEC_BATCH, DEC_SEQ, D_MODEL), 1.0),
        "cache_k": nrm(ks[2], (DEC_BATCH, DEPTH, NA_HEADS, PAST_LEN, HEAD_DIM), 1.0),
        "cache_v": nrm(ks[3], (DEC_BATCH, DEPTH, NA_HEADS, PAST_LEN, HEAD_DIM), 1.0),
        "c": nrm(ks[4], (DEC_BATCH, D_MODEL), 1.0),
        "c_ctx": nrm(ks[5], (D_MODEL,), 1.0),
        "ada_w": nrm(ks[6], (DEPTH, D_MODEL, N_MOD * D_MODEL), 0.5 * D_MODEL ** -0.5),
        "ada_b": nrm(ks[7], (DEPTH, N_MOD * D_MODEL), 0.02),
        "norm_g": 1.0 + nrm(ks[8], (DEPTH, 3, D_MODEL), 0.05),
        "ffn_w_gate": nrm(ks[9], (DEPTH, 2, D_MODEL, D_FF), D_MODEL ** -0.5),
        "ffn_w_up": nrm(ks[10], (DEPTH, 2, D_MODEL, D_FF), D_MODEL ** -0.5),
        "ffn_w_down": nrm(ks[11], (DEPTH, 2, D_FF, D_MODEL), D_FF ** -0.5),
        "w_in": nrm(ks[12], (DEPTH, D_MODEL, IN_WIDTH), D_MODEL ** -0.5),
        "pool_w": nrm(ks[13], (DEPTH, POOL_GROUPS, POOL_GROUP_DIM, POOL_GROUP_DIM), POOL_GROUP_DIM ** -0.5),
        "pool_scale": 1.0 + nrm(ks[14], (DEPTH, POOL_WIDTH), 0.05),
        "q_norm_g": 1.0 + nrm(ks[15], (DEPTH, HEAD_DIM), 0.05),
        "k_norm_g": 1.0 + nrm(ks[16], (DEPTH, HEAD_DIM), 0.05),
        "na_rpb": nrm(ks[17], (DEPTH, NA_HEADS, 2 * NA_ROWS_MAX - 1, 2 * NA_COLS - 1), 0.1),
        "sg_vnorm_g": 1.0 + nrm(ks[18], (DEPTH, SG_GROUPS, SG_GROUP_DIM), 0.05),
        "sg_w": nrm(ks[19], (DEPTH, SG_GROUPS, CHUNK, CHUNK), CHUNK ** -0.5),
        "sg_b": 1.0 + nrm(ks[20], (DEPTH, SG_GROUPS, CHUNK), 0.05),
        "out_norm_g": 1.0 + nrm(ks[21], (DEPTH, MIX_WIDTH), 0.05),
        "w_out": nrm(ks[22], (DEPTH, MIX_WIDTH, D_MODEL), MIX_WIDTH ** -0.5),
    }


def reference(x_prompt, x_sample, cache_k, cache_v, c, c_ctx, ada_w, ada_b, norm_g,
              ffn_w_gate, ffn_w_up, ffn_w_down, w_in, pool_w, pool_scale, q_norm_g,
              k_norm_g, na_rpb, sg_vnorm_g, sg_w, sg_b, out_norm_g, w_out):
    xp = x_prompt
    xs = x_sample
    ctx_k = []
    ctx_v = []
    for l in range(DEPTH):
        mc = (jax.nn.silu(c_ctx) @ ada_w[l] + ada_b[l]).reshape(N_MOD, D_MODEL)
        ml_all = (jax.nn.silu(c) @ ada_w[l] + ada_b[l]).reshape(c.shape[0], 1, N_MOD, D_MODEL)
        ml = [ml_all[:, :, i] for i in range(N_MOD)]

        xp = _half_ffn(xp, norm_g[l, 0], mc[0], mc[1], mc[2],
                       ffn_w_gate[l, 0], ffn_w_up[l, 0], ffn_w_down[l, 0])
        h = _modulate(_rms(xp, norm_g[l, 1]), mc[3], mc[4])
        za, q, k, v, zu, zsv = _mixer_inputs(h, w_in[l], q_norm_g[l], k_norm_g[l])
        ctx_k.append(k)
        ctx_v.append(v)
        o = _mixer_output(_pool_mixer(za, pool_w[l], pool_scale[l]),
                          _context_attention(q, k, v),
                          _chunk_mixer(zu, zsv, sg_vnorm_g[l], sg_w[l], sg_b[l]),
                          out_norm_g[l], w_out[l])
        xp = xp + mc[5] * o
        xp = _half_ffn(xp, norm_g[l, 2], mc[6], mc[7], mc[8],
                       ffn_w_gate[l, 1], ffn_w_up[l, 1], ffn_w_down[l, 1])

        xs = _half_ffn(xs, norm_g[l, 0], ml[0], ml[1], ml[2],
                       ffn_w_gate[l, 0], ffn_w_up[l, 0], ffn_w_down[l, 0])
        h = _modulate(_rms(xs, norm_g[l, 1]), ml[3], ml[4])
        za, q, k, v, zu, zsv = _mixer_inputs(h, w_in[l], q_norm_g[l], k_norm_g[l])
        o = _mixer_output(_pool_mixer(za, pool_w[l], pool_scale[l]),
                          _neighbourhood_attention(q, k, v, cache_k[:, l], cache_v[:, l], na_rpb[l]),
                          _chunk_mixer(zu, zsv, sg_vnorm_g[l], sg_w[l], sg_b[l]),
                          out_norm_g[l], w_out[l])
        xs = xs + ml[5] * o
        xs = _half_ffn(xs, norm_g[l, 2], ml[6], ml[7], ml[8],
                       ffn_w_gate[l, 1], ffn_w_up[l, 1], ffn_w_down[l, 1])

    new_k = jnp.stack(ctx_k, axis=1)
    new_v = jnp.stack(ctx_v, axis=1)
    return (xp, xs, new_k, new_v)
```

```python
import functools

import jax
import jax.numpy as jnp
from jax import lax
from jax.experimental import pallas as pl
from jax.experimental.pallas import tpu as pltpu

F32 = jnp.float32
BF16 = jnp.bfloat16

D_MODEL = 1024
D_FF = 2816
N_MOD = 9
DEPTH = 2
GROUP = 1024
N_GROUPS = 8
SEQ = 256
GRID_W = 64
GRID_ROWS = GROUP // GRID_W
POOL_WIDTH = 256
NA_WIDTH = 512
SG_WIDTH = 256
HEAD_DIM = 64
NA_HEADS = 8
HEAD_PAIRS = NA_HEADS // 2
PAIR_W = 2 * HEAD_DIM
NA_ROWS = 8
NA_COLS = 16
N_RPB_ROWS = 2 * NA_ROWS - 1
N_RPB_COLS = 2 * NA_COLS - 1
CHUNK = 128
PAST_LEN = 256
EPS = 1e-6
MASKED = -1e30

FFN_TM = 512
FFN_FC = 256
VMEM_LIMIT = 56 * 1024 * 1024


def _dot(a, b):
    return jnp.dot(a, b, preferred_element_type=F32)


def _dot_nt(a, b):
    return lax.dot_general(a, b, (((1,), (1,)), ((), ())), preferred_element_type=F32)


def _rms(x, gain):
    ms = jnp.mean(x * x, axis=-1, keepdims=True)
    return x * lax.rsqrt(ms + EPS) * gain


def _rms_mod(x, gain, shift, scale):
    return _rms(x, gain) * (1.0 + scale) + shift


def _params(sem):
    return pltpu.CompilerParams(dimension_semantics=sem, vmem_limit_bytes=VMEM_LIMIT)


def _resident(shape, index_map):
    return pl.BlockSpec(shape, index_map, pipeline_mode=pl.Buffered(1))


def _adaln_kernel(c_ref, w_ref, b_ref, o_ref):
    s = jax.nn.silu(c_ref[...]).astype(BF16)
    o_ref[0] = _dot(s, w_ref[0].astype(BF16)) + b_ref[0]


def _adaln(cvec, ada_w, ada_b):
    rows = cvec.shape[0]
    tn = 1024
    nt = (N_MOD * D_MODEL) // tn
    out = pl.pallas_call(
        _adaln_kernel,
        out_shape=jax.ShapeDtypeStruct((DEPTH, rows, N_MOD * D_MODEL), F32),
        grid=(DEPTH, nt),
        in_specs=[
            pl.BlockSpec((rows, D_MODEL), lambda l, n: (0, 0)),
            pl.BlockSpec((1, D_MODEL, tn), lambda l, n: (l, 0, n)),
            pl.BlockSpec((1, 1, tn), lambda l, n: (l, 0, n)),
        ],
        out_specs=pl.BlockSpec((1, rows, tn), lambda l, n: (l, 0, n)),
        compiler_params=_params(("arbitrary", "arbitrary")),
        name="adaln",
    )(cvec, ada_w, ada_b.reshape(DEPTH, 1, N_MOD * D_MODEL))
    return out.reshape(DEPTH, rows, N_MOD, D_MODEL)


def _ffn_tile(x, gain, shift, scale, gate, wg_ref, wu_ref, wd_ref):
    h = _rms_mod(x, gain, shift, scale).astype(BF16)
    acc = jnp.zeros(x.shape, F32)
    for c in range(D_FF // FFN_FC):
        lo = c * FFN_FC
        g = _dot(h, wg_ref[:, lo:lo + FFN_FC])
        u = _dot(h, wu_ref[:, lo:lo + FFN_FC])
        a = (g * jax.nn.sigmoid(g) * u).astype(BF16)
        acc = acc + _dot(a, wd_ref[lo:lo + FFN_FC, :])
    return x + 0.5 * gate * acc


def _ffn_kernel(x_ref, mod_ref, gain_ref, wg_ref, wu_ref, wd_ref, o_ref, *, mod0, gain_row):
    m = mod_ref[...]
    o_ref[...] = _ffn_tile(x_ref[...], gain_ref[gain_row:gain_row + 1, :],
                           m[mod0:mod0 + 1], m[mod0 + 1:mod0 + 2], m[mod0 + 2:mod0 + 3],
                           wg_ref, wu_ref, wd_ref)


def _mod_spec(layer, row0, per_group):
    return pl.BlockSpec((None, None, N_MOD, D_MODEL),
                        lambda g, *_: (layer, row0 + g * per_group, 0, 0))


def _ffn_weight_specs(layer, half):
    return [
        _resident((None, None, D_MODEL, D_FF), lambda *_: (layer, half, 0, 0)),
        _resident((None, None, D_MODEL, D_FF), lambda *_: (layer, half, 0, 0)),
        _resident((None, None, D_FF, D_MODEL), lambda *_: (layer, half, 0, 0)),
    ]


def _ffn(x, mods, norm_g, wg, wu, wd, *, layer, half, row0, per_group):
    n_t = GROUP // FFN_TM
    kern = functools.partial(_ffn_kernel, mod0=6 * half, gain_row=2 * half)
    return pl.pallas_call(
        kern,
        out_shape=jax.ShapeDtypeStruct(x.shape, F32),
        grid=(N_GROUPS, n_t),
        in_specs=[
            pl.BlockSpec((None, FFN_TM, D_MODEL), lambda g, t: (g, t, 0)),
            _mod_spec(layer, row0, per_group),
            _resident((None, 3, D_MODEL), lambda g, t: (layer, 0, 0)),
        ] + _ffn_weight_specs(layer, half),
        out_specs=pl.BlockSpec((None, FFN_TM, D_MODEL), lambda g, t: (g, t, 0)),
        compiler_params=_params(("arbitrary", "arbitrary")),
        name="ffn",
    )(x, mods, norm_g, wg, wu, wd)


def _pool_mixer(a, seq_len):
    t = a.shape[0]
    pos = lax.broadcasted_iota(jnp.int32, a.shape, 0) & (seq_len - 1)
    half = jnp.left_shift(1, lax.broadcasted_iota(jnp.int32, (1, a.shape[1]), 1) >> 6)
    right = a
    left = a
    for k in (1, 2, 4):
        grow = half >= 2 * k
        right = right + jnp.where(pos < jnp.where(grow, seq_len - k, 0),
                                  pltpu.roll(right, t - k, 0), 0.0)
        left = left + jnp.where(pos >= jnp.where(grow, k, seq_len),
                                pltpu.roll(left, k, 0), 0.0)
    window = right + jnp.where(pos >= 1, pltpu.roll(left, 1, 0), 0.0)
    cnt = jnp.minimum(pos + half, seq_len) - jnp.maximum(pos - half, 0)
    return window / cnt.astype(F32) - a


def _group_mean_sq(x, ones_ref):
    return _dot((x * x).astype(BF16), ones_ref[...]) * (1.0 / HEAD_DIM)


def _mixer_in_kernel(x_ref, mod_ref, gain_ref, w_in_ref, pool_w_ref, pool_s_ref,
                     qg_ref, kg_ref, vng_ref, sgw_ref, sgb_ref, og_ref, ones_ref,
                     oa_ref, q_ref, k_ref, v_ref, oc_ref, *kv_refs, seq_len):
    m = mod_ref[...]
    h = _rms_mod(x_ref[...], gain_ref[1:2, :], m[3:4], m[4:5]).astype(BF16)
    og = og_ref[...]

    za = _dot(h, w_in_ref[:, 0:POOL_WIDTH])
    pooled = _pool_mixer(za, seq_len)
    a_out = _dot(pooled.astype(BF16), pool_w_ref[...]) * pool_s_ref[...]
    oa_ref[...] = _rms(a_out, og[:, 0:POOL_WIDTH]).astype(BF16)

    c0 = POOL_WIDTH
    zq = _dot(h, w_in_ref[:, c0:c0 + NA_WIDTH])
    q = zq * lax.rsqrt(_group_mean_sq(zq, ones_ref) + EPS) * qg_ref[...]
    q_ref[...] = q.astype(BF16)
    zk = _dot(h, w_in_ref[:, c0 + NA_WIDTH:c0 + 2 * NA_WIDTH])
    k = zk * lax.rsqrt(_group_mean_sq(zk, ones_ref) + EPS) * kg_ref[...]
    k_ref[...] = k.astype(BF16)
    zv = _dot(h, w_in_ref[:, c0 + 2 * NA_WIDTH:c0 + 3 * NA_WIDTH])
    v_ref[...] = zv.astype(BF16)
    if kv_refs:
        kv_refs[0][...] = k
        kv_refs[1][...] = zv

    c1 = c0 + 3 * NA_WIDTH
    u = jax.nn.gelu(_dot(h, w_in_ref[:, c1:c1 + SG_WIDTH]))
    gv = jax.nn.gelu(_dot(h, w_in_ref[:, c1 + SG_WIDTH:c1 + 2 * SG_WIDTH]))
    ms = _group_mean_sq(gv, ones_ref.at[0:SG_WIDTH, 0:SG_WIDTH])
    vg = (gv * lax.rsqrt(ms + EPS) * vng_ref[...]).astype(BF16)
    lane_grp = lax.broadcasted_iota(jnp.int32, (CHUNK, SG_WIDTH), 1) >> 6
    sgw = sgw_ref[...]
    sgb = sgb_ref[...]
    gc = og[:, POOL_WIDTH + NA_WIDTH:]
    for n in range(GROUP // CHUNK):
        r0 = n * CHUNK
        full = _dot(sgw, vg[r0:r0 + CHUNK, :])
        sp = full[0:CHUNK]
        for gi in range(1, 4):
            sp = jnp.where(lane_grp == gi, full[gi * CHUNK:(gi + 1) * CHUNK], sp)
        c_out = u[r0:r0 + CHUNK, :] * (sp + sgb)
        oc_ref[r0:r0 + CHUNK, :] = _rms(c_out, gc).astype(BF16)


def _mixer_in(x, mods, norm_g, w_in, pool_w, pool_s, qg, kg, vng, sgw, sgb, og, ones,
              *, layer, row0, per_group, seq_len, emit_kv):
    kern = functools.partial(_mixer_in_kernel, seq_len=seq_len)
    tok = lambda w, dt: jax.ShapeDtypeStruct((N_GROUPS, GROUP, w), dt)
    tok_spec = lambda w: pl.BlockSpec((None, GROUP, w), lambda g: (g, 0, 0))
    lay = lambda *shape: _resident((None,) + shape, lambda g: (layer,) + (0,) * len(shape))
    kv_shapes = (tok(NA_WIDTH, F32), tok(NA_WIDTH, F32)) if emit_kv else ()
    kv_specs = (tok_spec(NA_WIDTH), tok_spec(NA_WIDTH)) if emit_kv else ()
    return pl.pallas_call(
        kern,
        out_shape=(tok(POOL_WIDTH, BF16), tok(NA_WIDTH, BF16), tok(NA_WIDTH, BF16),
                   tok(NA_WIDTH, BF16), tok(SG_WIDTH, BF16)) + kv_shapes,
        grid=(N_GROUPS,),
        in_specs=[
            tok_spec(D_MODEL),
            _mod_spec(layer, row0, per_group),
            lay(3, D_MODEL),
            lay(D_MODEL, w_in.shape[-1]),
            lay(POOL_WIDTH, POOL_WIDTH),
            lay(1, POOL_WIDTH),
            lay(1, NA_WIDTH),
            lay(1, NA_WIDTH),
            lay(1, SG_WIDTH),
            lay(4 * CHUNK, CHUNK),
            lay(CHUNK, SG_WIDTH),
            lay(1, D_MODEL),
            _resident((NA_WIDTH, NA_WIDTH), lambda g: (0, 0)),
        ],
        out_specs=(tok_spec(POOL_WIDTH), tok_spec(NA_WIDTH), tok_spec(NA_WIDTH),
                   tok_spec(NA_WIDTH), tok_spec(SG_WIDTH)) + kv_specs,
        compiler_params=_params(("arbitrary",)),
        name="mixer_in",
    )(x, mods, norm_g, w_in, pool_w, pool_s, qg, kg, vng, sgw, sgb, og, ones)


def _softmax_pv(scores, values):
    mx = scores[0].max(axis=-1, keepdims=True)
    for s in scores[1:]:
        mx = jnp.maximum(mx, s.max(axis=-1, keepdims=True))
    den = 0.0
    out = 0.0
    for s, v in zip(scores, values):
        e = jnp.exp(s - mx)
        den = den + e.sum(axis=-1, keepdims=True)
        out = out + _dot(e.astype(BF16), v)
    return out / den


def _ctx_attn_kernel(q_ref, k_ref, v_ref, o_ref):
    q = q_ref[...]
    k = k_ref[...]
    v = v_ref[...]
    first = lax.broadcasted_iota(jnp.int32, q.shape, 1) < HEAD_DIM
    zero = jnp.zeros_like(q)
    scale = HEAD_DIM ** -0.5
    o0 = _softmax_pv([_dot_nt(jnp.where(first, q, zero), k) * scale], [v])
    o1 = _softmax_pv([_dot_nt(jnp.where(first, zero, q), k) * scale], [v])
    o_ref[...] = jnp.where(first, o0, o1).astype(BF16)


def _ctx_attn(q, k, v):
    n_seq = GROUP // SEQ
    spec = pl.BlockSpec((None, SEQ, PAIR_W), lambda g, s, j: (g, s, j))
    return pl.pallas_call(
        _ctx_attn_kernel,
        out_shape=jax.ShapeDtypeStruct(q.shape, BF16),
        grid=(N_GROUPS, n_seq, HEAD_PAIRS),
        in_specs=[spec, spec, spec],
        out_specs=spec,
        compiler_params=_params(("arbitrary",) * 3),
        name="ctx_attn",
    )(q, k, v)


def _bias_kernel(rexp_ref, o_ref):
    shape = (GRID_W, 16 * GRID_W)
    cq = lax.broadcasted_iota(jnp.int32, shape, 0)
    ck = lax.broadcasted_iota(jnp.int32, shape, 1) & (GRID_W - 1)
    dc = jnp.clip(ck - cq, -(NA_COLS - 1), NA_COLS - 1) + (NA_COLS - 1)
    col_start = jnp.clip(cq - NA_COLS // 2, 0, GRID_W - NA_COLS)
    ok = (ck >= col_start) & (ck < col_start + NA_COLS)
    table = jnp.zeros(shape, F32)
    for j in range(N_RPB_COLS):
        table = jnp.where(dc == j, jnp.broadcast_to(rexp_ref[j:j + 1, :], shape), table)
    table = jnp.where(ok, table, MASKED)
    for d0 in range(NA_ROWS):
        o_ref[d0] = table[:, d0 * GRID_W:(d0 + NA_ROWS) * GRID_W]


def _bias_tables(rexp):
    return pl.pallas_call(
        _bias_kernel,
        out_shape=jax.ShapeDtypeStruct((DEPTH, NA_HEADS, NA_ROWS, GRID_W, NA_ROWS * GRID_W), F32),
        grid=(DEPTH, NA_HEADS),
        in_specs=[pl.BlockSpec((None, None, N_RPB_COLS + 1, 16 * GRID_W), lambda l, h: (l, h, 0, 0))],
        out_specs=pl.BlockSpec((None, None, NA_ROWS, GRID_W, NA_ROWS * GRID_W),
                               lambda l, h: (l, h, 0, 0, 0)),
        compiler_params=_params(("arbitrary", "arbitrary")),
        name="na_bias",
    )(rexp)


def _na_attn_kernel(q_ref, k_ref, v_ref, ck_ref, cv_ref, bias_ref, o_ref):
    kc = jnp.concatenate([ck_ref[0], ck_ref[1]], axis=-1).astype(BF16)
    vc = jnp.concatenate([cv_ref[0], cv_ref[1]], axis=-1).astype(BF16)
    first = lax.broadcasted_iota(jnp.int32, (GRID_W, PAIR_W), 1) < HEAD_DIM
    scale = HEAD_DIM ** -0.5
    n_loc = NA_ROWS * GRID_W

    def row(r, carry):
        start = jnp.clip(r - NA_ROWS // 2, 0, GRID_ROWS - NA_ROWS)
        d0 = start - r + (NA_ROWS - 1)
        q = q_ref[pl.ds(pl.multiple_of(r * GRID_W, GRID_W), GRID_W), :]
        base = pl.multiple_of(start * GRID_W, GRID_W)
        k_loc = k_ref[pl.ds(base, n_loc), :]
        v_loc = v_ref[pl.ds(base, n_loc), :]
        zero = jnp.zeros_like(q)
        outs = []
        for e in range(2):
            qm = jnp.where(first, q, zero) if e == 0 else jnp.where(first, zero, q)
            s_loc = _dot_nt(qm, k_loc) * scale + bias_ref[e, d0]
            s_ctx = _dot_nt(qm, kc) * scale
            outs.append(_softmax_pv([s_loc, s_ctx], [v_loc, vc]))
        o_ref[pl.ds(pl.multiple_of(r * GRID_W, GRID_W), GRID_W), :] = (
            jnp.where(first, outs[0], outs[1]).astype(BF16))
        return carry

    lax.fori_loop(0, GRID_ROWS, row, 0)


def _na_attn(q, k, v, cache_k, cache_v, bias, *, layer):
    spec = pl.BlockSpec((None, GROUP, PAIR_W), lambda b, j: (b, 0, j))
    cspec = pl.BlockSpec((None, None, 2, PAST_LEN, HEAD_DIM), lambda b, j: (b, layer, j, 0, 0))
    bspec = pl.BlockSpec((None, 2, NA_ROWS, GRID_W, NA_ROWS * GRID_W),
                         lambda b, j: (layer, j, 0, 0, 0))
    return pl.pallas_call(
        _na_attn_kernel,
        out_shape=jax.ShapeDtypeStruct(q.shape, BF16),
        grid=(N_GROUPS, HEAD_PAIRS),
        in_specs=[spec, spec, spec, cspec, cspec, bspec],
        out_specs=spec,
        compiler_params=_params(("arbitrary", "arbitrary")),
        name="na_attn",
    )(q, k, v, cache_k, cache_v, bias)


def _mixer_out_ffn_kernel(x_ref, oa_ref, ob_ref, oc_ref, mod_ref, gain_ref, og_ref, wo_ref,
                          wg_ref, wu_ref, wd_ref, o_ref):
    m = mod_ref[...]
    ob = ob_ref[...].astype(F32)
    gb = og_ref[:, POOL_WIDTH:POOL_WIDTH + NA_WIDTH]
    ob = _rms(ob, gb).astype(BF16)
    c0 = POOL_WIDTH
    c1 = POOL_WIDTH + NA_WIDTH
    o = (_dot(oa_ref[...], wo_ref[0:c0, :]) + _dot(ob, wo_ref[c0:c1, :])
         + _dot(oc_ref[...], wo_ref[c1:, :]))
    x = x_ref[...] + m[5:6] * o
    o_ref[...] = _ffn_tile(x, gain_ref[2:3, :], m[6:7], m[7:8], m[8:9], wg_ref, wu_ref, wd_ref)


def _mixer_out_ffn(x, oa, ob, oc, mods, norm_g, og, wo, wg, wu, wd, *, layer, row0, per_group):
    n_t = GROUP // FFN_TM
    tok_spec = lambda w: pl.BlockSpec((None, FFN_TM, w), lambda g, t: (g, t, 0))
    return pl.pallas_call(
        _mixer_out_ffn_kernel,
        out_shape=jax.ShapeDtypeStruct(x.shape, F32),
        grid=(N_GROUPS, n_t),
        in_specs=[
            tok_spec(D_MODEL), tok_spec(POOL_WIDTH), tok_spec(NA_WIDTH), tok_spec(SG_WIDTH),
            _mod_spec(layer, row0, per_group),
            _resident((None, 3, D_MODEL), lambda g, t: (layer, 0, 0)),
            _resident((None, 1, D_MODEL), lambda g, t: (layer, 0, 0)),
            _resident((None, D_MODEL, D_MODEL), lambda g, t: (layer, 0, 0)),
        ] + _ffn_weight_specs(layer, 1),
        out_specs=tok_spec(D_MODEL),
        compiler_params=_params(("arbitrary", "arbitrary")),
        name="mixer_out_ffn",
    )(x, oa, ob, oc, mods, norm_g, og, wo, wg, wu, wd)


def _block_diag(w):
    out = jnp.zeros((DEPTH, POOL_WIDTH, POOL_WIDTH), w.dtype)
    for g in range(4):
        lo = g * HEAD_DIM
        out = out.at[:, lo:lo + HEAD_DIM, lo:lo + HEAD_DIM].set(w[:, g])
    return out


def _heads_major(x, batch):
    return x.reshape(batch, SEQ, NA_HEADS, HEAD_DIM).transpose(0, 2, 1, 3)


def kernel(x_prompt, x_sample, cache_k, cache_v, c, c_ctx, ada_w, ada_b, norm_g, ffn_w_gate,
           ffn_w_up, ffn_w_down, w_in, pool_w, pool_scale, q_norm_g, k_norm_g, na_rpb,
           sg_vnorm_g, sg_w, sg_b, out_norm_g, w_out):
    batch = x_prompt.shape[0]
    dec_batch = x_sample.shape[0]

    cvec = jnp.concatenate([c_ctx[None], c, jnp.zeros((16 - 1 - dec_batch, D_MODEL), F32)])
    mods = _adaln(cvec, ada_w, ada_b)

    wg = ffn_w_gate.astype(BF16)
    wu = ffn_w_up.astype(BF16)
    wd = ffn_w_down.astype(BF16)
    w_in_b = w_in.astype(BF16)
    wo = w_out.astype(BF16)
    pool_wb = _block_diag(pool_w).astype(BF16)
    pool_s = pool_scale.reshape(DEPTH, 1, POOL_WIDTH)
    qg = jnp.tile(q_norm_g, (1, NA_HEADS)).reshape(DEPTH, 1, NA_WIDTH)
    kg = jnp.tile(k_norm_g, (1, NA_HEADS)).reshape(DEPTH, 1, NA_WIDTH)
    vng = sg_vnorm_g.reshape(DEPTH, 1, SG_WIDTH)
    sgw = sg_w.reshape(DEPTH, 4 * CHUNK, CHUNK).astype(BF16)
    sgb = jnp.repeat(jnp.swapaxes(sg_b, 1, 2), HEAD_DIM, axis=-1)
    og = out_norm_g.reshape(DEPTH, 1, D_MODEL)
    lane = jnp.arange(NA_WIDTH) // HEAD_DIM
    ones = (lane[:, None] == lane[None, :]).astype(BF16)
    rexp = jnp.repeat(jnp.swapaxes(na_rpb, 2, 3), GRID_W, axis=-1)
    rexp = jnp.pad(rexp, ((0, 0), (0, 0), (0, 1), (0, GRID_W)))
    bias = _bias_tables(rexp)

    xp = x_prompt.reshape(N_GROUPS, GROUP, D_MODEL)
    xs = x_sample.reshape(N_GROUPS, GROUP, D_MODEL)
    new_k = []
    new_v = []
    for l in range(DEPTH):
        lw = dict(layer=l)
        for path in range(2):
            x = xp if path == 0 else xs
            rows = dict(row0=path, per_group=path)
            x = _ffn(x, mods, norm_g, wg, wu, wd, half=0, **lw, **rows)
            oa, q, k, v, oc, *kv = _mixer_in(
                x, mods, norm_g, w_in_b, pool_wb, pool_s, qg, kg, vng, sgw, sgb, og, ones,
                seq_len=SEQ if path == 0 else GROUP, emit_kv=path == 0, **lw, **rows)
            if path == 0:
                new_k.append(_heads_major(kv[0], batch))
                new_v.append(_heads_major(kv[1], batch))
                ob = _ctx_attn(q, k, v)
            else:
                ob = _na_attn(q, k, v, cache_k, cache_v, bias, **lw)
            x = _mixer_out_ffn(x, oa, ob, oc, mods, norm_g, og, wo, wg, wu, wd, **lw, **rows)
            if path == 0:
                xp = x
            else:
                xs = x

    return (xp.reshape(x_prompt.shape), xs.reshape(x_sample.shape),
            jnp.stack(new_k, axis=1), jnp.stack(new_v, axis=1))
```

```python
import functools

import jax
import jax.numpy as jnp
from jax import lax
from jax.experimental import pallas as pl
from jax.experimental.pallas import tpu as pltpu

F32 = jnp.float32
BF16 = jnp.bfloat16

D_MODEL = 1024
D_FF = 2816
N_MOD = 9
DEPTH = 2
GROUP = 1024
N_GROUPS = 8
SEQ = 256
GRID_W = 64
GRID_ROWS = GROUP // GRID_W
POOL_WIDTH = 256
NA_WIDTH = 512
SG_WIDTH = 256
HEAD_DIM = 64
NA_HEADS = 8
HEAD_PAIRS = NA_HEADS // 2
PAIR_W = 2 * HEAD_DIM
NA_ROWS = 8
NA_COLS = 16
N_RPB_ROWS = 2 * NA_ROWS - 1
N_RPB_COLS = 2 * NA_COLS - 1
CHUNK = 128
PAST_LEN = 256
EPS = 1e-6
MASKED = -1e30

FFN_TM = 512
FFN_FC = 256
VMEM_LIMIT = 56 * 1024 * 1024


def _dot(a, b):
    return jnp.dot(a, b, preferred_element_type=F32)


def _dot_nt(a, b):
    return lax.dot_general(a, b, (((1,), (1,)), ((), ())), preferred_element_type=F32)


def _rms(x, gain):
    ms = jnp.mean(x * x, axis=-1, keepdims=True)
    return x * lax.rsqrt(ms + EPS) * gain


def _rms_mod(x, gain, shift, scale):
    return _rms(x, gain) * (1.0 + scale) + shift


def _params(sem):
    return pltpu.CompilerParams(dimension_semantics=sem, vmem_limit_bytes=VMEM_LIMIT)


def _resident(shape, index_map):
    return pl.BlockSpec(shape, index_map, pipeline_mode=pl.Buffered(1))


def _adaln_kernel(c_ref, w_ref, b_ref, o_ref):
    s = jax.nn.silu(c_ref[...]).astype(BF16)
    o_ref[0] = _dot(s, w_ref[0].astype(BF16)) + b_ref[0]


def _adaln(cvec, ada_w, ada_b):
    rows = cvec.shape[0]
    tn = 1024
    nt = (N_MOD * D_MODEL) // tn
    out = pl.pallas_call(
        _adaln_kernel,
        out_shape=jax.ShapeDtypeStruct((DEPTH, rows, N_MOD * D_MODEL), F32),
        grid=(DEPTH, nt),
        in_specs=[
            pl.BlockSpec((rows, D_MODEL), lambda l, n: (0, 0)),
            pl.BlockSpec((1, D_MODEL, tn), lambda l, n: (l, 0, n)),
            pl.BlockSpec((1, 1, tn), lambda l, n: (l, 0, n)),
        ],
        out_specs=pl.BlockSpec((1, rows, tn), lambda l, n: (l, 0, n)),
        compiler_params=_params(("arbitrary", "arbitrary")),
        name="adaln",
    )(cvec, ada_w, ada_b.reshape(DEPTH, 1, N_MOD * D_MODEL))
    return out.reshape(DEPTH, rows, N_MOD, D_MODEL)


def _ffn_tile(x, gain, shift, scale, gate, wg_ref, wu_ref, wd_ref):
    h = _rms_mod(x, gain, shift, scale).astype(BF16)
    acc = jnp.zeros(x.shape, F32)
    for c in range(D_FF // FFN_FC):
        lo = c * FFN_FC
        g = _dot(h, wg_ref[:, lo:lo + FFN_FC])
        u = _dot(h, wu_ref[:, lo:lo + FFN_FC])
        a = (g * jax.nn.sigmoid(g) * u).astype(BF16)
        acc = acc + _dot(a, wd_ref[lo:lo + FFN_FC, :])
    return x + 0.5 * gate * acc


def _ffn_kernel(x_ref, mod_ref, gain_ref, wg_ref, wu_ref, wd_ref, o_ref, *, mod0, gain_row):
    m = mod_ref[...]
    o_ref[...] = _ffn_tile(x_ref[...], gain_ref[gain_row:gain_row + 1, :],
                           m[mod0:mod0 + 1], m[mod0 + 1:mod0 + 2], m[mod0 + 2:mod0 + 3],
                           wg_ref, wu_ref, wd_ref)


def _mod_spec(layer, row0, per_group):
    return pl.BlockSpec((None, None, N_MOD, D_MODEL),
                        lambda g, *_: (layer, row0 + g * per_group, 0, 0))


def _ffn_weight_specs(layer, half):
    return [
        _resident((None, None, D_MODEL, D_FF), lambda *_: (layer, half, 0, 0)),
        _resident((None, None, D_MODEL, D_FF), lambda *_: (layer, half, 0, 0)),
        _resident((None, None, D_FF, D_MODEL), lambda *_: (layer, half, 0, 0)),
    ]


def _ffn(x, mods, norm_g, wg, wu, wd, *, layer, half, row0, per_group):
    n_t = GROUP // FFN_TM
    kern = functools.partial(_ffn_kernel, mod0=6 * half, gain_row=2 * half)
    return pl.pallas_call(
        kern,
        out_shape=jax.ShapeDtypeStruct(x.shape, F32),
        grid=(N_GROUPS, n_t),
        in_specs=[
            pl.BlockSpec((None, FFN_TM, D_MODEL), lambda g, t: (g, t, 0)),
            _mod_spec(layer, row0, per_group),
            _resident((None, 3, D_MODEL), lambda g, t: (layer, 0, 0)),
        ] + _ffn_weight_specs(layer, half),
        out_specs=pl.BlockSpec((None, FFN_TM, D_MODEL), lambda g, t: (g, t, 0)),
        compiler_params=_params(("arbitrary", "arbitrary")),
        name="ffn",
    )(x, mods, norm_g, wg, wu, wd)


def _pool_mixer(a, seq_len):
    t = a.shape[0]
    pos = lax.broadcasted_iota(jnp.int32, a.shape, 0) & (seq_len - 1)
    half = jnp.left_shift(1, lax.broadcasted_iota(jnp.int32, (1, a.shape[1]), 1) >> 6)
    right = a
    left = a
    for k in (1, 2, 4):
        grow = half >= 2 * k
        right = right + jnp.where(pos < jnp.where(grow, seq_len - k, 0),
                                  pltpu.roll(right, t - k, 0), 0.0)
        left = left + jnp.where(pos >= jnp.where(grow, k, seq_len),
                                pltpu.roll(left, k, 0), 0.0)
    window = right + jnp.where(pos >= 1, pltpu.roll(left, 1, 0), 0.0)
    cnt = jnp.minimum(pos + half, seq_len) - jnp.maximum(pos - half, 0)
    return window / cnt.astype(F32) - a


def _group_mean_sq(x, ones_ref):
    return _dot((x * x).astype(BF16), ones_ref[...]) * (1.0 / HEAD_DIM)


def _mixer_in_kernel(x_ref, mod_ref, gain_ref, w_in_ref, pool_w_ref, pool_s_ref,
                     qg_ref, kg_ref, vng_ref, sgw_ref, sgb_ref, og_ref, ones_ref,
                     *refs, seq_len, n_alias):
    oa_ref, q_ref, k_ref, v_ref, oc_ref, *kv_refs = refs[n_alias:]
    m = mod_ref[...]
    h = _rms_mod(x_ref[...], gain_ref[1:2, :], m[3:4], m[4:5]).astype(BF16)
    og = og_ref[...]

    za = _dot(h, w_in_ref[:, 0:POOL_WIDTH])
    pooled = _pool_mixer(za, seq_len)
    a_out = _dot(pooled.astype(BF16), pool_w_ref[...]) * pool_s_ref[...]
    oa_ref[...] = _rms(a_out, og[:, 0:POOL_WIDTH]).astype(BF16)

    c0 = POOL_WIDTH
    zq = _dot(h, w_in_ref[:, c0:c0 + NA_WIDTH])
    q = zq * lax.rsqrt(_group_mean_sq(zq, ones_ref) + EPS) * qg_ref[...]
    q_ref[...] = q.astype(BF16)
    zk = _dot(h, w_in_ref[:, c0 + NA_WIDTH:c0 + 2 * NA_WIDTH])
    k = zk * lax.rsqrt(_group_mean_sq(zk, ones_ref) + EPS) * kg_ref[...]
    k_ref[...] = k.astype(BF16)
    zv = _dot(h, w_in_ref[:, c0 + 2 * NA_WIDTH:c0 + 3 * NA_WIDTH])
    v_ref[...] = zv.astype(BF16)
    for val, ref in zip((k, zv), kv_refs):
        for s in range(GROUP // SEQ):
            for hd in range(NA_HEADS):
                ref[s, hd] = val[s * SEQ:(s + 1) * SEQ, hd * HEAD_DIM:(hd + 1) * HEAD_DIM]

    c1 = c0 + 3 * NA_WIDTH
    u = jax.nn.gelu(_dot(h, w_in_ref[:, c1:c1 + SG_WIDTH]))
    gv = jax.nn.gelu(_dot(h, w_in_ref[:, c1 + SG_WIDTH:c1 + 2 * SG_WIDTH]))
    ms = _group_mean_sq(gv, ones_ref.at[0:SG_WIDTH, 0:SG_WIDTH])
    vg = (gv * lax.rsqrt(ms + EPS) * vng_ref[...]).astype(BF16)
    lane_grp = lax.broadcasted_iota(jnp.int32, (CHUNK, SG_WIDTH), 1) >> 6
    sgw = sgw_ref[...]
    sgb = sgb_ref[...]
    gc = og[:, POOL_WIDTH + NA_WIDTH:]
    for n in range(GROUP // CHUNK):
        r0 = n * CHUNK
        full = _dot(sgw, vg[r0:r0 + CHUNK, :])
        sp = full[0:CHUNK]
        for gi in range(1, 4):
            sp = jnp.where(lane_grp == gi, full[gi * CHUNK:(gi + 1) * CHUNK], sp)
        c_out = u[r0:r0 + CHUNK, :] * (sp + sgb)
        oc_ref[r0:r0 + CHUNK, :] = _rms(c_out, gc).astype(BF16)


def _mixer_in(x, mods, norm_g, w_in, pool_w, pool_s, qg, kg, vng, sgw, sgb, og, ones,
              *, layer, row0, per_group, seq_len, emit_kv, prev_kv=()):
    kern = functools.partial(_mixer_in_kernel, seq_len=seq_len, n_alias=len(prev_kv))
    tok = lambda w, dt: jax.ShapeDtypeStruct((N_GROUPS, GROUP, w), dt)
    tok_spec = lambda w: pl.BlockSpec((None, GROUP, w), lambda g: (g, 0, 0))
    lay = lambda *shape: _resident((None,) + shape, lambda g: (layer,) + (0,) * len(shape))
    n_seq = GROUP // SEQ
    kv_shape = jax.ShapeDtypeStruct((N_GROUPS * n_seq, DEPTH, NA_HEADS, SEQ, HEAD_DIM), F32)
    kv_spec = pl.BlockSpec((n_seq, None, NA_HEADS, SEQ, HEAD_DIM), lambda g: (g, layer, 0, 0, 0))
    kv_shapes = (kv_shape, kv_shape) if emit_kv else ()
    kv_specs = (kv_spec, kv_spec) if emit_kv else ()
    n_in = 13
    aliases = {n_in + i: 5 + i for i in range(len(prev_kv))}
    return pl.pallas_call(
        kern,
        out_shape=(tok(POOL_WIDTH, BF16), tok(NA_WIDTH, BF16), tok(NA_WIDTH, BF16),
                   tok(NA_WIDTH, BF16), tok(SG_WIDTH, BF16)) + kv_shapes,
        grid=(N_GROUPS,),
        in_specs=[
            tok_spec(D_MODEL),
            _mod_spec(layer, row0, per_group),
            lay(3, D_MODEL),
            lay(D_MODEL, w_in.shape[-1]),
            lay(POOL_WIDTH, POOL_WIDTH),
            lay(1, POOL_WIDTH),
            lay(1, NA_WIDTH),
            lay(1, NA_WIDTH),
            lay(1, SG_WIDTH),
            lay(4 * CHUNK, CHUNK),
            lay(CHUNK, SG_WIDTH),
            lay(1, D_MODEL),
            _resident((NA_WIDTH, NA_WIDTH), lambda g: (0, 0)),
        ] + [pl.BlockSpec(memory_space=pl.ANY)] * len(prev_kv),
        out_specs=(tok_spec(POOL_WIDTH), tok_spec(NA_WIDTH), tok_spec(NA_WIDTH),
                   tok_spec(NA_WIDTH), tok_spec(SG_WIDTH)) + kv_specs,
        input_output_aliases=aliases,
        compiler_params=_params(("arbitrary",)),
        name="mixer_in",
    )(x, mods, norm_g, w_in, pool_w, pool_s, qg, kg, vng, sgw, sgb, og, ones, *prev_kv)


def _softmax_pv(scores, values):
    mx = scores[0].max(axis=-1, keepdims=True)
    for s in scores[1:]:
        mx = jnp.maximum(mx, s.max(axis=-1, keepdims=True))
    den = 0.0
    out = 0.0
    for s, v in zip(scores, values):
        e = jnp.exp(s - mx)
        den = den + e.sum(axis=-1, keepdims=True)
        out = out + _dot(e.astype(BF16), v)
    return out / den


def _head_queries(q):
    first = lax.broadcasted_iota(jnp.int32, q.shape, 1) < HEAD_DIM
    qs = q.astype(F32) * (HEAD_DIM ** -0.5)
    return (jnp.where(first, qs, 0.0).astype(BF16), jnp.where(first, 0.0, qs).astype(BF16)), first


def _ctx_attn_kernel(q_ref, k_ref, v_ref, o_ref):
    for s in range(GROUP // SEQ):
        rows = slice(s * SEQ, (s + 1) * SEQ)
        k = k_ref[rows, :]
        v = v_ref[rows, :]
        (q0, q1), first = _head_queries(q_ref[rows, :])
        o0 = _softmax_pv([_dot_nt(q0, k)], [v])
        o1 = _softmax_pv([_dot_nt(q1, k)], [v])
        o_ref[rows, :] = jnp.where(first, o0, o1).astype(BF16)


def _ctx_attn(q, k, v):
    spec = pl.BlockSpec((None, GROUP, PAIR_W), lambda g, j: (g, 0, j))
    return pl.pallas_call(
        _ctx_attn_kernel,
        out_shape=jax.ShapeDtypeStruct(q.shape, BF16),
        grid=(N_GROUPS, HEAD_PAIRS),
        in_specs=[spec, spec, spec],
        out_specs=spec,
        compiler_params=_params(("arbitrary", "arbitrary")),
        name="ctx_attn",
    )(q, k, v)


NA_QROWS = 4


def _na_plan():
    blocks, tables, offsets, width = [], [], [], 0
    for r0 in range(0, GRID_ROWS, NA_QROWS):
        rs = range(r0, r0 + NA_QROWS)
        starts = [min(max(r - NA_ROWS // 2, 0), GRID_ROWS - NA_ROWS) for r in rs]
        nk = max(starts) + NA_ROWS - min(starts)
        nk += nk % 2
        k0 = min(min(starts), GRID_ROWS - nk)
        table = (nk, tuple((k0 - r + NA_ROWS - 1, s - k0, s - k0 + NA_ROWS)
                           for r, s in zip(rs, starts)))
        assert all(d >= 0 and d + nk <= N_RPB_ROWS + 1 for d, _, _ in table[1])
        if table not in tables:
            tables.append(table)
            offsets.append(width)
            width += nk * GRID_W
        blocks.append((r0, k0, nk, offsets[tables.index(table)]))
    return tuple(blocks), tuple(tables), width


def _bias_kernel(rexp_ref, o_ref, *, tables):
    shape = (GRID_W, 16 * GRID_W)
    cq = lax.broadcasted_iota(jnp.int32, shape, 0)
    ck = lax.broadcasted_iota(jnp.int32, shape, 1) & (GRID_W - 1)
    dc = jnp.clip(ck - cq, -(NA_COLS - 1), NA_COLS - 1) + (NA_COLS - 1)
    col_start = jnp.clip(cq - NA_COLS // 2, 0, GRID_W - NA_COLS)
    ok = (ck >= col_start) & (ck < col_start + NA_COLS)
    table = jnp.zeros(shape, F32)
    for j in range(N_RPB_COLS):
        table = jnp.where(dc == j, jnp.broadcast_to(rexp_ref[j:j + 1, :], shape), table)
    table = jnp.where(ok, table, MASKED)
    col0 = 0
    for nk, rows in tables:
        width = nk * GRID_W
        krow = lax.broadcasted_iota(jnp.int32, (GRID_W, width), 1) >> 6
        for rl, (d, lo, hi) in enumerate(rows):
            slab = table[:, d * GRID_W:d * GRID_W + width]
            if lo > 0 or hi < nk:
                slab = jnp.where((krow >= lo) & (krow < hi), slab, MASKED)
            o_ref[rl * GRID_W:(rl + 1) * GRID_W, col0:col0 + width] = slab
        col0 += width


def _bias_tables(rexp, tables, width):
    return pl.pallas_call(
        functools.partial(_bias_kernel, tables=tables),
        out_shape=jax.ShapeDtypeStruct((DEPTH, NA_HEADS, NA_QROWS * GRID_W, width), F32),
        grid=(DEPTH, NA_HEADS),
        in_specs=[pl.BlockSpec((None, None, N_RPB_COLS + 1, 16 * GRID_W), lambda l, h: (l, h, 0, 0))],
        out_specs=pl.BlockSpec((None, None, NA_QROWS * GRID_W, width), lambda l, h: (l, h, 0, 0)),
        compiler_params=_params(("arbitrary", "arbitrary")),
        name="na_bias",
    )(rexp)


def _na_attn_kernel(q_ref, k_ref, v_ref, ck_ref, cv_ref, bias_ref, o_ref, *, blocks):
    kc = jnp.concatenate([ck_ref[0], ck_ref[1]], axis=-1).astype(BF16)
    vc = jnp.concatenate([cv_ref[0], cv_ref[1]], axis=-1).astype(BF16)
    for r0, k0, nk, col0 in blocks:
        q_rows = slice(r0 * GRID_W, (r0 + NA_QROWS) * GRID_W)
        k_rows = slice(k0 * GRID_W, (k0 + nk) * GRID_W)
        k_loc = k_ref[k_rows, :]
        v_loc = v_ref[k_rows, :]
        qs, first = _head_queries(q_ref[q_rows, :])
        outs = []
        for e in range(2):
            s_loc = _dot_nt(qs[e], k_loc) + bias_ref[e, :, col0:col0 + nk * GRID_W]
            s_ctx = _dot_nt(qs[e], kc)
            outs.append(_softmax_pv([s_loc, s_ctx], [v_loc, vc]))
        o_ref[q_rows, :] = jnp.where(first, outs[0], outs[1]).astype(BF16)


def _na_attn(q, k, v, cache_k, cache_v, bias, blocks, *, layer):
    spec = pl.BlockSpec((None, GROUP, PAIR_W), lambda j, b: (b, 0, j))
    cspec = pl.BlockSpec((None, None, 2, PAST_LEN, HEAD_DIM), lambda j, b: (b, layer, j, 0, 0))
    bspec = pl.BlockSpec((None, 2) + bias.shape[2:], lambda j, b: (layer, j, 0, 0))
    return pl.pallas_call(
        functools.partial(_na_attn_kernel, blocks=blocks),
        out_shape=jax.ShapeDtypeStruct(q.shape, BF16),
        grid=(HEAD_PAIRS, N_GROUPS),
        in_specs=[spec, spec, spec, cspec, cspec, bspec],
        out_specs=spec,
        compiler_params=_params(("arbitrary", "arbitrary")),
        name="na_attn",
    )(q, k, v, cache_k, cache_v, bias)


def _mixer_out_ffn_kernel(x_ref, oa_ref, ob_ref, oc_ref, mod_ref, gain_ref, og_ref, wo_ref,
                          wg_ref, wu_ref, wd_ref, o_ref):
    m = mod_ref[...]
    ob = ob_ref[...].astype(F32)
    gb = og_ref[:, POOL_WIDTH:POOL_WIDTH + NA_WIDTH]
    ob = _rms(ob, gb).astype(BF16)
    c0 = POOL_WIDTH
    c1 = POOL_WIDTH + NA_WIDTH
    o = (_dot(oa_ref[...], wo_ref[0:c0, :]) + _dot(ob, wo_ref[c0:c1, :])
         + _dot(oc_ref[...], wo_ref[c1:, :]))
    x = x_ref[...] + m[5:6] * o
    o_ref[...] = _ffn_tile(x, gain_ref[2:3, :], m[6:7], m[7:8], m[8:9], wg_ref, wu_ref, wd_ref)


def _mixer_out_ffn(x, oa, ob, oc, mods, norm_g, og, wo, wg, wu, wd, *, layer, row0, per_group):
    n_t = GROUP // FFN_TM
    tok_spec = lambda w: pl.BlockSpec((None, FFN_TM, w), lambda g, t: (g, t, 0))
    return pl.pallas_call(
        _mixer_out_ffn_kernel,
        out_shape=jax.ShapeDtypeStruct(x.shape, F32),
        grid=(N_GROUPS, n_t),
        in_specs=[
            tok_spec(D_MODEL), tok_spec(POOL_WIDTH), tok_spec(NA_WIDTH), tok_spec(SG_WIDTH),
            _mod_spec(layer, row0, per_group),
            _resident((None, 3, D_MODEL), lambda g, t: (layer, 0, 0)),
            _resident((None, 1, D_MODEL), lambda g, t: (layer, 0, 0)),
            _resident((None, D_MODEL, D_MODEL), lambda g, t: (layer, 0, 0)),
        ] + _ffn_weight_specs(layer, 1),
        out_specs=tok_spec(D_MODEL),
        compiler_params=_params(("arbitrary", "arbitrary")),
        name="mixer_out_ffn",
    )(x, oa, ob, oc, mods, norm_g, og, wo, wg, wu, wd)


def _block_diag(w):
    out = jnp.zeros((DEPTH, POOL_WIDTH, POOL_WIDTH), w.dtype)
    for g in range(4):
        lo = g * HEAD_DIM
        out = out.at[:, lo:lo + HEAD_DIM, lo:lo + HEAD_DIM].set(w[:, g])
    return out


def kernel(x_prompt, x_sample, cache_k, cache_v, c, c_ctx, ada_w, ada_b, norm_g, ffn_w_gate,
           ffn_w_up, ffn_w_down, w_in, pool_w, pool_scale, q_norm_g, k_norm_g, na_rpb,
           sg_vnorm_g, sg_w, sg_b, out_norm_g, w_out):
    dec_batch = x_sample.shape[0]

    cvec = jnp.concatenate([c_ctx[None], c, jnp.zeros((16 - 1 - dec_batch, D_MODEL), F32)])
    mods = _adaln(cvec, ada_w, ada_b)

    wg = ffn_w_gate.astype(BF16)
    wu = ffn_w_up.astype(BF16)
    wd = ffn_w_down.astype(BF16)
    w_in_b = w_in.astype(BF16)
    wo = w_out.astype(BF16)
    pool_wb = _block_diag(pool_w).astype(BF16)
    pool_s = pool_scale.reshape(DEPTH, 1, POOL_WIDTH)
    qg = jnp.tile(q_norm_g, (1, NA_HEADS)).reshape(DEPTH, 1, NA_WIDTH)
    kg = jnp.tile(k_norm_g, (1, NA_HEADS)).reshape(DEPTH, 1, NA_WIDTH)
    vng = sg_vnorm_g.reshape(DEPTH, 1, SG_WIDTH)
    sgw = sg_w.reshape(DEPTH, 4 * CHUNK, CHUNK).astype(BF16)
    sgb = jnp.repeat(jnp.swapaxes(sg_b, 1, 2), HEAD_DIM, axis=-1)
    og = out_norm_g.reshape(DEPTH, 1, D_MODEL)
    lane = jnp.arange(NA_WIDTH) // HEAD_DIM
    ones = (lane[:, None] == lane[None, :]).astype(BF16)
    rexp = jnp.repeat(jnp.swapaxes(na_rpb, 2, 3), GRID_W, axis=-1)
    rexp = jnp.pad(rexp, ((0, 0), (0, 0), (0, 1), (0, GRID_W)))
    na_blocks, na_tables, na_width = _na_plan()
    bias = _bias_tables(rexp, na_tables, na_width)

    xp = x_prompt.reshape(N_GROUPS, GROUP, D_MODEL)
    xs = x_sample.reshape(N_GROUPS, GROUP, D_MODEL)
    new_kv = ()
    for l in range(DEPTH):
        lw = dict(layer=l)
        for path in range(2):
            x = xp if path == 0 else xs
            rows = dict(row0=path, per_group=path)
            x = _ffn(x, mods, norm_g, wg, wu, wd, half=0, **lw, **rows)
            oa, q, k, v, oc, *kv = _mixer_in(
                x, mods, norm_g, w_in_b, pool_wb, pool_s, qg, kg, vng, sgw, sgb, og, ones,
                seq_len=SEQ if path == 0 else GROUP, emit_kv=path == 0,
                prev_kv=new_kv if path == 0 else (), **lw, **rows)
            if path == 0:
                new_kv = tuple(kv)
                ob = _ctx_attn(q, k, v)
            else:
                ob = _na_attn(q, k, v, cache_k, cache_v, bias, na_blocks, **lw)
            x = _mixer_out_ffn(x, oa, ob, oc, mods, norm_g, og, wo, wg, wu, wd, **lw, **rows)
            if path == 0:
                xp = x
            else:
                xs = x

    return (xp.reshape(x_prompt.shape), xs.reshape(x_sample.shape)) + new_kv
```

```python
import functools

import jax
import jax.numpy as jnp
from jax import lax
from jax.experimental import pallas as pl
from jax.experimental.pallas import tpu as pltpu

F32 = jnp.float32
BF16 = jnp.bfloat16

D_MODEL = 1024
D_FF = 2816
N_MOD = 9
DEPTH = 2
GROUP = 1024
N_GROUPS = 8
SEQ = 256
GRID_W = 64
GRID_ROWS = GROUP // GRID_W
POOL_WIDTH = 256
NA_WIDTH = 512
SG_WIDTH = 256
HEAD_DIM = 64
NA_HEADS = 8
HEAD_PAIRS = NA_HEADS // 2
PAIR_W = 2 * HEAD_DIM
NA_ROWS = 8
NA_COLS = 16
N_RPB_ROWS = 2 * NA_ROWS - 1
N_RPB_COLS = 2 * NA_COLS - 1
CHUNK = 128
PAST_LEN = 256
EPS = 1e-6
MASKED = -1e30
LOG2E = 1.4426950408889634
QUERY_SCALE = HEAD_DIM ** -0.5 * LOG2E

FFN_TM = 512
FFN_FC = 256
MIX_TM = 512
VMEM_LIMIT = 56 * 1024 * 1024


def _dot(a, b):
    return jnp.dot(a, b, preferred_element_type=F32)


def _dot_nt(a, b):
    return lax.dot_general(a, b, (((1,), (1,)), ((), ())), preferred_element_type=F32)


def _rms(x, gain):
    ms = jnp.mean(x * x, axis=-1, keepdims=True)
    return x * lax.rsqrt(ms + EPS) * gain


def _rms_mod(x, gain, shift, scale):
    ms = jnp.mean(x * x, axis=-1, keepdims=True)
    return x * lax.rsqrt(ms + EPS) * (gain * (1.0 + scale)) + shift


def _params(sem):
    return pltpu.CompilerParams(dimension_semantics=sem, vmem_limit_bytes=VMEM_LIMIT)


def _resident(shape, index_map):
    return pl.BlockSpec(shape, index_map, pipeline_mode=pl.Buffered(1))


def _adaln_kernel(c_ref, w_ref, b_ref, o_ref):
    s = jax.nn.silu(c_ref[...]).astype(BF16)
    o_ref[0] = _dot(s, w_ref[0].astype(BF16)) + b_ref[0]


def _adaln(cvec, ada_w, ada_b):
    rows = cvec.shape[0]
    tn = 1024
    nt = (N_MOD * D_MODEL) // tn
    out = pl.pallas_call(
        _adaln_kernel,
        out_shape=jax.ShapeDtypeStruct((DEPTH, rows, N_MOD * D_MODEL), F32),
        grid=(DEPTH, nt),
        in_specs=[
            pl.BlockSpec((rows, D_MODEL), lambda l, n: (0, 0)),
            pl.BlockSpec((1, D_MODEL, tn), lambda l, n: (l, 0, n)),
            pl.BlockSpec((1, 1, tn), lambda l, n: (l, 0, n)),
        ],
        out_specs=pl.BlockSpec((1, rows, tn), lambda l, n: (l, 0, n)),
        compiler_params=_params(("arbitrary", "arbitrary")),
        name="adaln",
    )(cvec, ada_w, ada_b.reshape(DEPTH, 1, N_MOD * D_MODEL))
    return out.reshape(DEPTH, rows, N_MOD, D_MODEL)


def _ffn_tile(x, gain, shift, scale, gate, wg_ref, wu_ref, wd_ref, side_jobs=()):
    h = _rms_mod(x, gain, shift, scale).astype(BF16)
    acc = jnp.zeros(x.shape, F32)
    n_chunks = D_FF // FFN_FC
    per_chunk = -(-len(side_jobs) // n_chunks)
    for c in range(n_chunks):
        finishers = [job() for job in side_jobs[c * per_chunk:(c + 1) * per_chunk]]
        lo = c * FFN_FC
        g = _dot(h, wg_ref[:, lo:lo + FFN_FC])
        u = _dot(h, wu_ref[:, lo:lo + FFN_FC])
        a = (g * jax.nn.sigmoid(g) * u).astype(BF16)
        acc = acc + _dot(a, wd_ref[lo:lo + FFN_FC, :])
        for finish in finishers:
            finish()
    return x + 0.5 * gate * acc


def _ffn_kernel(x_ref, mod_ref, gain_ref, wg_ref, wu_ref, wd_ref, o_ref, *, mod0, gain_row,
                side_jobs=()):
    m = mod_ref[...]
    o_ref[...] = _ffn_tile(x_ref[...], gain_ref[gain_row:gain_row + 1, :],
                           m[mod0:mod0 + 1], m[mod0 + 1:mod0 + 2], m[mod0 + 2:mod0 + 3],
                           wg_ref, wu_ref, wd_ref, side_jobs)


def _mod_spec(layer, row0, per_group, group_axis=0):
    return pl.BlockSpec((None, None, N_MOD, D_MODEL),
                        lambda *idx: (layer, row0 + idx[group_axis] * per_group, 0, 0))


def _ffn_weight_specs(layer, half):
    return [
        _resident((None, None, D_MODEL, D_FF), lambda *_: (layer, half, 0, 0)),
        _resident((None, None, D_MODEL, D_FF), lambda *_: (layer, half, 0, 0)),
        _resident((None, None, D_FF, D_MODEL), lambda *_: (layer, half, 0, 0)),
    ]


N_TILES = GROUP // FFN_TM
SIDE_PAIRS = HEAD_PAIRS // N_TILES


def _tile_spec(width):
    return pl.BlockSpec((None, FFN_TM, width), lambda t, g: (g, t, 0))


def _joint_kernel(*refs, main, n_main, side, n_side):
    outs = refs[n_main + n_side:]
    jobs = () if side is None else side(*refs[n_main:n_main + n_side], outs[1])
    main(*refs[:n_main], outs[0], side_jobs=jobs)


def _tile_call(name, main, main_args, main_specs, x_shape, side=None):
    side_kernel, side_args, side_specs, side_shape, side_spec = side or (None, (), [], None, None)
    kern = functools.partial(_joint_kernel, main=main, n_main=len(main_args),
                             side=side_kernel, n_side=len(side_args))
    out_shape = [jax.ShapeDtypeStruct(x_shape, F32)]
    out_specs = [_tile_spec(D_MODEL)]
    if side is not None:
        out_shape.append(side_shape)
        out_specs.append(side_spec)
    outs = pl.pallas_call(
        kern,
        out_shape=out_shape,
        grid=(N_TILES, N_GROUPS),
        in_specs=list(main_specs) + list(side_specs),
        out_specs=out_specs,
        compiler_params=_params(("arbitrary", "arbitrary")),
        name=name,
    )(*main_args, *side_args)
    return outs[0] if side is None else outs


def _ffn(x, mods, norm_g, wg, wu, wd, *, layer, half, row0, per_group, side=None):
    kern = functools.partial(_ffn_kernel, mod0=6 * half, gain_row=2 * half)
    specs = [
        _tile_spec(D_MODEL),
        _mod_spec(layer, row0, per_group, group_axis=1),
        _resident((None, 3, D_MODEL), lambda t, g: (layer, 0, 0)),
    ] + _ffn_weight_specs(layer, half)
    return _tile_call("ffn", kern, (x, mods, norm_g, wg, wu, wd), specs, x.shape, side)


LANES = 128
GROUP_COLS = 256


def _pool_column(a, lane0, seq_len):
    t = a.shape[0]
    pos = lax.broadcasted_iota(jnp.int32, a.shape, 0) & (seq_len - 1)
    lane = lax.broadcasted_iota(jnp.int32, (1, LANES), 1) + lane0
    half = jnp.left_shift(1, lane >> 6)
    max_half = 1 << ((lane0 + LANES - 1) // HEAD_DIM)
    right = a
    left = a
    k = 1
    while 2 * k <= max_half:
        grow = half >= 2 * k
        right = right + jnp.where(pos < jnp.where(grow, seq_len - k, 0),
                                  pltpu.roll(right, t - k, 0), 0.0)
        left = left + jnp.where(pos >= jnp.where(grow, k, seq_len),
                                pltpu.roll(left, k, 0), 0.0)
        k *= 2
    window = right + jnp.where(pos >= 1, pltpu.roll(left, 1, 0), 0.0)
    cnt = jnp.minimum(pos + half, seq_len) - jnp.maximum(pos - half, 0)
    return window / cnt.astype(F32) - a


def _pool_mixer(a, seq_len):
    cols = [_pool_column(a[:, c:c + LANES], c, seq_len) for c in range(0, a.shape[1], LANES)]
    return jnp.concatenate(cols, axis=-1)


def _group_rms(z, ones_ref):
    outs = []
    for c in range(z.shape[1] // GROUP_COLS):
        zc = z[:, c * GROUP_COLS:(c + 1) * GROUP_COLS]
        ms = _dot((zc * zc).astype(BF16), ones_ref[...]) * (1.0 / HEAD_DIM)
        outs.append(zc * lax.rsqrt(ms + EPS))
    return outs[0] if len(outs) == 1 else jnp.concatenate(outs, axis=-1)


def _mixer_in_kernel(x_ref, mod_ref, gain_ref, w_in_ref, pool_w_ref, pool_s_ref,
                     qg_ref, kg_ref, vng_ref, sgw_ref, sgb_ref, og_ref, ones_ref,
                     *refs, seq_len, n_alias):
    oa_ref, q_ref, k_ref, v_ref, oc_ref, *kv_refs = refs[n_alias:]
    m = mod_ref[...]
    og = og_ref[...]
    gc = og[:, POOL_WIDTH + NA_WIDTH:]
    lane_grp = lax.broadcasted_iota(jnp.int32, (CHUNK, SG_WIDTH), 1) >> 6
    sgw = sgw_ref[...]
    sgb = sgb_ref[...]
    splits = (0, POOL_WIDTH, POOL_WIDTH + NA_WIDTH, POOL_WIDTH + 2 * NA_WIDTH,
              POOL_WIDTH + 3 * NA_WIDTH, POOL_WIDTH + 3 * NA_WIDTH + SG_WIDTH, w_in_ref.shape[1])

    def pool_out(rows, za):
        pooled = _pool_mixer(za, seq_len)
        a_out = _dot(pooled.astype(BF16), pool_w_ref[...]) * pool_s_ref[...]
        oa_ref[rows, :] = _rms(a_out, og[:, 0:POOL_WIDTH]).astype(BF16)

    tiles = [slice(r, r + MIX_TM) for r in range(0, GROUP, MIX_TM)]
    projected = []
    for rows in tiles:
        h = _rms_mod(x_ref[rows, :], gain_ref[1:2, :], m[3:4], m[4:5]).astype(BF16)
        projected.append([_dot(h, w_in_ref[:, a:b]) for a, b in zip(splits[:-1], splits[1:])])

    for rows, (za, zq, zk, zv, zu, zsv) in zip(tiles, projected):
        if seq_len <= MIX_TM:
            pool_out(rows, za)

        q_ref[rows, :] = (_group_rms(zq, ones_ref) * (qg_ref[...] * QUERY_SCALE)).astype(BF16)
        k = _group_rms(zk, ones_ref) * kg_ref[...]
        k_ref[rows, :] = k.astype(BF16)
        v_ref[rows, :] = zv.astype(BF16)
        for val, ref in zip((k, zv), kv_refs):
            for s in range(MIX_TM // SEQ):
                for hd in range(NA_HEADS):
                    ref[rows.start // SEQ + s, hd] = val[s * SEQ:(s + 1) * SEQ,
                                                         hd * HEAD_DIM:(hd + 1) * HEAD_DIM]

        u = jax.nn.gelu(zu)
        vg = (_group_rms(jax.nn.gelu(zsv), ones_ref) * vng_ref[...]).astype(BF16)
        for r0 in range(0, MIX_TM, CHUNK):
            full = _dot(sgw, vg[r0:r0 + CHUNK, :])
            sp = full[0:CHUNK]
            for gi in range(1, 4):
                sp = jnp.where(lane_grp == gi, full[gi * CHUNK:(gi + 1) * CHUNK], sp)
            c_out = u[r0:r0 + CHUNK, :] * (sp + sgb)
            oc_ref[rows.start + r0:rows.start + r0 + CHUNK, :] = _rms(c_out, gc).astype(BF16)

    if seq_len > MIX_TM:
        pool_out(slice(0, GROUP), jnp.concatenate([z[0] for z in projected], axis=0))


def _mixer_in(x, mods, norm_g, w_in, pool_w, pool_s, qg, kg, vng, sgw, sgb, og, ones,
              *, layer, row0, per_group, seq_len, emit_kv, prev_kv=()):
    kern = functools.partial(_mixer_in_kernel, seq_len=seq_len, n_alias=len(prev_kv))
    tok = lambda w, dt: jax.ShapeDtypeStruct((N_GROUPS, GROUP, w), dt)
    tok_spec = lambda w: pl.BlockSpec((None, GROUP, w), lambda g: (g, 0, 0))
    lay = lambda *shape: _resident((None,) + shape, lambda g: (layer,) + (0,) * len(shape))
    n_seq = GROUP // SEQ
    kv_shape = jax.ShapeDtypeStruct((N_GROUPS * n_seq, DEPTH, NA_HEADS, SEQ, HEAD_DIM), F32)
    kv_spec = pl.BlockSpec((n_seq, None, NA_HEADS, SEQ, HEAD_DIM), lambda g: (g, layer, 0, 0, 0))
    kv_shapes = (kv_shape, kv_shape) if emit_kv else ()
    kv_specs = (kv_spec, kv_spec) if emit_kv else ()
    n_in = 13
    aliases = {n_in + i: 5 + i for i in range(len(prev_kv))}
    return pl.pallas_call(
        kern,
        out_shape=(tok(POOL_WIDTH, BF16), tok(NA_WIDTH, BF16), tok(NA_WIDTH, BF16),
                   tok(NA_WIDTH, BF16), tok(SG_WIDTH, BF16)) + kv_shapes,
        grid=(N_GROUPS,),
        in_specs=[
            tok_spec(D_MODEL),
            _mod_spec(layer, row0, per_group),
            lay(3, D_MODEL),
            lay(D_MODEL, w_in.shape[-1]),
            lay(POOL_WIDTH, POOL_WIDTH),
            lay(1, POOL_WIDTH),
            lay(1, NA_WIDTH),
            lay(1, NA_WIDTH),
            lay(1, SG_WIDTH),
            lay(4 * CHUNK, CHUNK),
            lay(CHUNK, SG_WIDTH),
            lay(1, D_MODEL),
            _resident((GROUP_COLS, GROUP_COLS), lambda g: (0, 0)),
        ] + [pl.BlockSpec(memory_space=pl.ANY)] * len(prev_kv),
        out_specs=(tok_spec(POOL_WIDTH), tok_spec(NA_WIDTH), tok_spec(NA_WIDTH),
                   tok_spec(NA_WIDTH), tok_spec(SG_WIDTH)) + kv_specs,
        input_output_aliases=aliases,
        compiler_params=_params(("arbitrary",)),
        name="mixer_in",
    )(x, mods, norm_g, w_in, pool_w, pool_s, qg, kg, vng, sgw, sgb, og, ones, *prev_kv)


def _first_head(shape):
    return lax.broadcasted_iota(jnp.int32, shape, 1) < HEAD_DIM


def _head_queries(q):
    first = _first_head(q.shape)
    zero = jnp.zeros_like(q)
    return jnp.where(first, q, zero), jnp.where(first, zero, q)


def _head_values(v):
    first = _first_head(v.shape)
    one = jnp.ones_like(v)
    return jnp.where(first, v, one), jnp.where(first, one, v)


def _softmax_pv(scores, values):
    mx = scores[0].max(axis=-1, keepdims=True)
    for s in scores[1:]:
        mx = jnp.maximum(mx, s.max(axis=-1, keepdims=True))
    out = 0.0
    for s, v in zip(scores, values):
        out = out + _dot(jnp.exp2(s - mx).astype(BF16), v)
    return out


def _merge_heads(o0, o1):
    first = _first_head(o0.shape)
    num = jnp.where(first, o0, o1)
    den = pltpu.roll(jnp.where(first, o1, o0), HEAD_DIM, 1)
    return (num / den).astype(BF16)


def _ctx_attn_jobs(q_ref, k_ref, v_ref, o_ref):
    def job(rows, lanes):
        def start():
            k = k_ref[rows, lanes]
            q0, q1 = _head_queries(q_ref[rows, lanes])
            s0, s1 = _dot_nt(q0, k), _dot_nt(q1, k)

            def finish():
                v0, v1 = _head_values(v_ref[rows, lanes])
                o_ref[rows, lanes] = _merge_heads(_softmax_pv([s0], [v0]),
                                                  _softmax_pv([s1], [v1]))
            return finish
        return start

    return [job(slice(s, s + SEQ), slice(p, p + PAIR_W))
            for p in range(0, q_ref.shape[1], PAIR_W) for s in range(0, GROUP, SEQ)]


def _side_spec():
    return pl.BlockSpec((None, GROUP, SIDE_PAIRS * PAIR_W), lambda t, g: (g, 0, t))


def _ctx_attn_side(q, k, v):
    spec = _side_spec()
    return (_ctx_attn_jobs, (q, k, v), [spec, spec, spec],
            jax.ShapeDtypeStruct(q.shape, BF16), spec)


NA_QROWS = 4


def _na_plan():
    blocks, tables, offsets, width = [], [], [], 0
    for r0 in range(0, GRID_ROWS, NA_QROWS):
        rs = range(r0, r0 + NA_QROWS)
        starts = [min(max(r - NA_ROWS // 2, 0), GRID_ROWS - NA_ROWS) for r in rs]
        nk = max(starts) + NA_ROWS - min(starts)
        nk += nk % 2
        k0 = min(min(starts), GRID_ROWS - nk)
        table = (nk, tuple((k0 - r + NA_ROWS - 1, s - k0, s - k0 + NA_ROWS)
                           for r, s in zip(rs, starts)))
        assert all(d >= 0 and d + nk <= N_RPB_ROWS + 1 for d, _, _ in table[1])
        if table not in tables:
            tables.append(table)
            offsets.append(width)
            width += nk * GRID_W
        blocks.append((r0, k0, nk, offsets[tables.index(table)]))
    return tuple(blocks), tuple(tables), width


def _bias_kernel(rexp_ref, o_ref, *, tables):
    shape = (GRID_W, 16 * GRID_W)
    cq = lax.broadcasted_iota(jnp.int32, shape, 0)
    ck = lax.broadcasted_iota(jnp.int32, shape, 1) & (GRID_W - 1)
    dc = jnp.clip(ck - cq, -(NA_COLS - 1), NA_COLS - 1) + (NA_COLS - 1)
    col_start = jnp.clip(cq - NA_COLS // 2, 0, GRID_W - NA_COLS)
    ok = (ck >= col_start) & (ck < col_start + NA_COLS)
    table = jnp.zeros(shape, F32)
    for j in range(N_RPB_COLS):
        table = jnp.where(dc == j, jnp.broadcast_to(rexp_ref[j:j + 1, :], shape), table)
    table = jnp.where(ok, table * LOG2E, MASKED)
    col0 = 0
    for nk, rows in tables:
        width = nk * GRID_W
        krow = lax.broadcasted_iota(jnp.int32, (GRID_W, width), 1) >> 6
        for rl, (d, lo, hi) in enumerate(rows):
            slab = table[:, d * GRID_W:d * GRID_W + width]
            if lo > 0 or hi < nk:
                slab = jnp.where((krow >= lo) & (krow < hi), slab, MASKED)
            o_ref[rl * GRID_W:(rl + 1) * GRID_W, col0:col0 + width] = slab
        col0 += width


def _bias_tables(rexp, tables, width):
    return pl.pallas_call(
        functools.partial(_bias_kernel, tables=tables),
        out_shape=jax.ShapeDtypeStruct((DEPTH, NA_HEADS, NA_QROWS * GRID_W, width), F32),
        grid=(DEPTH, NA_HEADS),
        in_specs=[pl.BlockSpec((None, None, N_RPB_COLS + 1, 16 * GRID_W), lambda l, h: (l, h, 0, 0))],
        out_specs=pl.BlockSpec((None, None, NA_QROWS * GRID_W, width), lambda l, h: (l, h, 0, 0)),
        compiler_params=_params(("arbitrary", "arbitrary")),
        name="na_bias",
    )(rexp)


def _na_attn_jobs(q_ref, k_ref, v_ref, ck_ref, cv_ref, bias_ref, o_ref, *, blocks):
    pair_cache = {}

    def cached_pair(ref, p):
        key = (id(ref), p)
        if key not in pair_cache:
            pair_cache[key] = jnp.concatenate([ref[2 * p], ref[2 * p + 1]], axis=-1).astype(BF16)
        return pair_cache[key]

    def job(p, r0, k0, nk, col0):
        lanes = slice(p * PAIR_W, (p + 1) * PAIR_W)
        h0 = 2 * p
        q_rows = slice(r0 * GRID_W, (r0 + NA_QROWS) * GRID_W)
        k_rows = slice(k0 * GRID_W, (k0 + nk) * GRID_W)

        def start():
            kc = cached_pair(ck_ref, p)
            k_loc = k_ref[k_rows, lanes]
            scores = []
            for e, qe in enumerate(_head_queries(q_ref[q_rows, lanes])):
                s_loc = _dot_nt(qe, k_loc) + bias_ref[h0 + e, :, col0:col0 + nk * GRID_W]
                scores.append([s_loc, _dot_nt(qe, kc)])

            def finish():
                vc = _head_values(cached_pair(cv_ref, p))
                v_loc = _head_values(v_ref[k_rows, lanes])
                o_ref[q_rows, lanes] = _merge_heads(
                    *[_softmax_pv(scores[e], [v_loc[e], vc[e]]) for e in range(2)])
            return finish
        return start

    return [job(p, *blk) for p in range(q_ref.shape[1] // PAIR_W) for blk in blocks]


def _na_attn_side(q, k, v, cache_k, cache_v, bias, blocks, *, layer):
    spec = _side_spec()
    n_heads = 2 * SIDE_PAIRS
    cspec = pl.BlockSpec((None, None, n_heads, PAST_LEN, HEAD_DIM),
                         lambda t, g: (g, layer, t, 0, 0))
    bspec = _resident((None, n_heads) + bias.shape[2:], lambda t, g: (layer, t, 0, 0))
    return (functools.partial(_na_attn_jobs, blocks=blocks),
            (q, k, v, cache_k, cache_v, bias), [spec, spec, spec, cspec, cspec, bspec],
            jax.ShapeDtypeStruct(q.shape, BF16), spec)


def _mixer_out_ffn_kernel(x_ref, oa_ref, ob_ref, oc_ref, mod_ref, gain_ref, og_ref, wo_ref,
                          wg_ref, wu_ref, wd_ref, o_ref, *, side_jobs=()):
    m = mod_ref[...]
    gb = og_ref[:, POOL_WIDTH:POOL_WIDTH + NA_WIDTH]
    c0 = POOL_WIDTH
    c1 = POOL_WIDTH + NA_WIDTH
    ob = _rms(ob_ref[...].astype(F32), gb).astype(BF16)
    o = (_dot(oa_ref[...], wo_ref[0:c0, :]) + _dot(ob, wo_ref[c0:c1, :])
         + _dot(oc_ref[...], wo_ref[c1:, :]))
    x = x_ref[...] + m[5:6] * o
    o_ref[...] = _ffn_tile(x, gain_ref[2:3, :], m[6:7], m[7:8], m[8:9], wg_ref, wu_ref, wd_ref,
                           side_jobs)


def _mixer_out_ffn(x, oa, ob, oc, mods, norm_g, og, wo, wg, wu, wd, *, layer, row0, per_group,
                   side=None):
    specs = [
        _tile_spec(D_MODEL), _tile_spec(POOL_WIDTH), _tile_spec(NA_WIDTH), _tile_spec(SG_WIDTH),
        _mod_spec(layer, row0, per_group, group_axis=1),
        _resident((None, 3, D_MODEL), lambda t, g: (layer, 0, 0)),
        _resident((None, 1, D_MODEL), lambda t, g: (layer, 0, 0)),
        _resident((None, D_MODEL, D_MODEL), lambda t, g: (layer, 0, 0)),
    ] + _ffn_weight_specs(layer, 1)
    return _tile_call("mixer_out_ffn", _mixer_out_ffn_kernel,
                      (x, oa, ob, oc, mods, norm_g, og, wo, wg, wu, wd), specs, x.shape, side)


def _block_diag(w):
    out = jnp.zeros((DEPTH, POOL_WIDTH, POOL_WIDTH), w.dtype)
    for g in range(4):
        lo = g * HEAD_DIM
        out = out.at[:, lo:lo + HEAD_DIM, lo:lo + HEAD_DIM].set(w[:, g])
    return out


def kernel(x_prompt, x_sample, cache_k, cache_v, c, c_ctx, ada_w, ada_b, norm_g, ffn_w_gate,
           ffn_w_up, ffn_w_down, w_in, pool_w, pool_scale, q_norm_g, k_norm_g, na_rpb,
           sg_vnorm_g, sg_w, sg_b, out_norm_g, w_out):
    dec_batch = x_sample.shape[0]

    cvec = jnp.concatenate([c_ctx[None], c, jnp.zeros((16 - 1 - dec_batch, D_MODEL), F32)])
    mods = _adaln(cvec, ada_w, ada_b)

    wg = ffn_w_gate.astype(BF16)
    wu = ffn_w_up.astype(BF16)
    wd = ffn_w_down.astype(BF16)
    w_in_b = w_in.astype(BF16)
    wo = w_out.astype(BF16)
    pool_wb = _block_diag(pool_w).astype(BF16)
    pool_s = pool_scale.reshape(DEPTH, 1, POOL_WIDTH)
    qg = jnp.tile(q_norm_g, (1, NA_HEADS)).reshape(DEPTH, 1, NA_WIDTH)
    kg = jnp.tile(k_norm_g, (1, NA_HEADS)).reshape(DEPTH, 1, NA_WIDTH)
    vng = sg_vnorm_g.reshape(DEPTH, 1, SG_WIDTH)
    sgw = sg_w.reshape(DEPTH, 4 * CHUNK, CHUNK).astype(BF16)
    sgb = jnp.repeat(jnp.swapaxes(sg_b, 1, 2), HEAD_DIM, axis=-1)
    og = out_norm_g.reshape(DEPTH, 1, D_MODEL)
    lane = jnp.arange(GROUP_COLS) // HEAD_DIM
    ones = (lane[:, None] == lane[None, :]).astype(BF16)
    rexp = jnp.repeat(jnp.swapaxes(na_rpb, 2, 3), GRID_W, axis=-1)
    rexp = jnp.pad(rexp, ((0, 0), (0, 0), (0, 1), (0, GRID_W)))
    na_blocks, na_tables, na_width = _na_plan()
    bias = _bias_tables(rexp, na_tables, na_width)

    xp = x_prompt.reshape(N_GROUPS, GROUP, D_MODEL)
    xs = x_sample.reshape(N_GROUPS, GROUP, D_MODEL)
    ctx = dict(row0=0, per_group=0)
    lat = dict(row0=1, per_group=1)
    new_kv = ()
    for l in range(DEPTH):
        mix = (mods, norm_g, w_in_b, pool_wb, pool_s, qg, kg, vng, sgw, sgb, og, ones)
        xp = _ffn(xp, mods, norm_g, wg, wu, wd, layer=l, half=0, **ctx)
        oa_p, q, k, v, oc_p, *new_kv = _mixer_in(xp, *mix, layer=l, seq_len=SEQ, emit_kv=True,
                                                 prev_kv=new_kv, **ctx)
        xs, ob_p = _ffn(xs, mods, norm_g, wg, wu, wd, layer=l, half=0, **lat,
                        side=_ctx_attn_side(q, k, v))
        oa_s, q, k, v, oc_s = _mixer_in(xs, *mix, layer=l, seq_len=GROUP, emit_kv=False, **lat)
        xp, ob_s = _mixer_out_ffn(xp, oa_p, ob_p, oc_p, mods, norm_g, og, wo, wg, wu, wd,
                                  layer=l, **ctx,
                                  side=_na_attn_side(q, k, v, cache_k, cache_v, bias, na_blocks,
                                                     layer=l))
        xs = _mixer_out_ffn(xs, oa_s, ob_s, oc_s, mods, norm_g, og, wo, wg, wu, wd,
                            layer=l, **lat)

    return (xp.reshape(x_prompt.shape), xs.reshape(x_sample.shape)) + tuple(new_kv)
```

```python
import functools

import jax
import jax.numpy as jnp
from jax import lax
from jax.experimental import pallas as pl
from jax.experimental.pallas import tpu as pltpu

F32 = jnp.float32
BF16 = jnp.bfloat16

D_MODEL = 1024
D_FF = 2816
N_MOD = 9
DEPTH = 2
GROUP = 1024
N_GROUPS = 8
SEQ = 256
GRID_W = 64
GRID_ROWS = GROUP // GRID_W
POOL_WIDTH = 256
NA_WIDTH = 512
SG_WIDTH = 256
HEAD_DIM = 64
NA_HEADS = 8
HEAD_PAIRS = NA_HEADS // 2
PAIR_W = 2 * HEAD_DIM
NA_ROWS = 8
NA_COLS = 16
N_RPB_ROWS = 2 * NA_ROWS - 1
N_RPB_COLS = 2 * NA_COLS - 1
CHUNK = 128
PAST_LEN = 256
EPS = 1e-6
MASKED = -1e30
LOG2E = 1.4426950408889634
QUERY_SCALE = HEAD_DIM ** -0.5 * LOG2E

FFN_TM = 512
FFN_FC = 256
MIX_TM = 512
VMEM_LIMIT = 56 * 1024 * 1024


def _dot(a, b):
    return jnp.dot(a, b, preferred_element_type=F32)


def _dot_nt(a, b):
    return lax.dot_general(a, b, (((1,), (1,)), ((), ())), preferred_element_type=F32)


def _rms(x, gain):
    ms = jnp.mean(x * x, axis=-1, keepdims=True)
    return x * lax.rsqrt(ms + EPS) * gain


def _rms_mod(x, gain, shift, scale):
    ms = jnp.mean(x * x, axis=-1, keepdims=True)
    return x * lax.rsqrt(ms + EPS) * (gain * (1.0 + scale)) + shift


def _params(sem):
    return pltpu.CompilerParams(dimension_semantics=sem, vmem_limit_bytes=VMEM_LIMIT)


def _resident(shape, index_map):
    return pl.BlockSpec(shape, index_map, pipeline_mode=pl.Buffered(1))


ADALN_TN = 1024


def _adaln_kernel(c_ref, w_ref, b_ref, o_ref):
    s = jax.nn.silu(c_ref[...]).astype(BF16)
    o_ref[...] = _dot(s, w_ref[...].astype(BF16)) + b_ref[...]


def _adaln(cvec, ada_w, ada_b):
    rows = cvec.shape[0]
    out = pl.pallas_call(
        _adaln_kernel,
        out_shape=jax.ShapeDtypeStruct((DEPTH, rows, N_MOD * D_MODEL), F32),
        grid=(DEPTH, N_MOD * D_MODEL // ADALN_TN),
        in_specs=[
            pl.BlockSpec((rows, D_MODEL), lambda l, n: (0, 0)),
            pl.BlockSpec((None, D_MODEL, ADALN_TN), lambda l, n: (l, 0, n)),
            pl.BlockSpec((None, 1, ADALN_TN), lambda l, n: (l, 0, n)),
        ],
        out_specs=pl.BlockSpec((None, rows, ADALN_TN), lambda l, n: (l, 0, n)),
        compiler_params=_params(("arbitrary", "arbitrary")),
        name="adaln",
    )(cvec, ada_w, ada_b.reshape(DEPTH, 1, N_MOD * D_MODEL))
    return out.reshape(DEPTH, rows, N_MOD, D_MODEL)


def _ffn_tile(x, gain, shift, scale, gate, wg_ref, wu_ref, wd_ref, side_jobs=()):
    h = _rms_mod(x, gain, shift, scale).astype(BF16)
    acc = jnp.zeros(x.shape, F32)
    n_chunks = D_FF // FFN_FC
    per_chunk = -(-len(side_jobs) // n_chunks)
    for c in range(n_chunks):
        finishers = [job() for job in side_jobs[c * per_chunk:(c + 1) * per_chunk]]
        lo = c * FFN_FC
        g = _dot(h, wg_ref[:, lo:lo + FFN_FC])
        u = _dot(h, wu_ref[:, lo:lo + FFN_FC])
        a = (g * jax.nn.sigmoid(g) * u).astype(BF16)
        acc = acc + _dot(a, wd_ref[lo:lo + FFN_FC, :])
        for finish in finishers:
            finish()
    return x + 0.5 * gate * acc


def _ffn_kernel(x_ref, mod_ref, gain_ref, wg_ref, wu_ref, wd_ref, o_ref, *, mod0, gain_row,
                side_jobs=()):
    m = mod_ref[...]
    o_ref[...] = _ffn_tile(x_ref[...], gain_ref[gain_row:gain_row + 1, :],
                           m[mod0:mod0 + 1], m[mod0 + 1:mod0 + 2], m[mod0 + 2:mod0 + 3],
                           wg_ref, wu_ref, wd_ref, side_jobs)


def _mod_spec(row0, per_group, group_axis=0):
    return pl.BlockSpec((None, N_MOD, D_MODEL),
                        lambda *idx: (row0 + idx[group_axis] * per_group, 0, 0))


def _ffn_weight_specs(half):
    return [
        _resident((None, D_MODEL, D_FF), lambda *_: (half, 0, 0)),
        _resident((None, D_MODEL, D_FF), lambda *_: (half, 0, 0)),
        _resident((None, D_FF, D_MODEL), lambda *_: (half, 0, 0)),
    ]


N_TILES = GROUP // FFN_TM
N_STEPS = N_TILES * N_GROUPS
SIDE_PAIRS = HEAD_PAIRS // N_TILES


def _tile_spec(width):
    return pl.BlockSpec((None, FFN_TM, width), lambda t, g: (g, t, 0))


def _step(t, g):
    return t * N_GROUPS + g


def _one_stage(fn):
    def start():
        fn()
        return lambda: None
    return start


def _joint_kernel(*refs, main, n_main, sides):
    outs = refs[n_main + sum(n_in for _, n_in, _ in sides):]
    jobs = []
    i, o = n_main, 1
    for jobs_fn, n_in, n_out in sides:
        jobs += jobs_fn(*refs[i:i + n_in], *outs[o:o + n_out])
        i, o = i + n_in, o + n_out
    main(*refs[:n_main], outs[0], side_jobs=jobs)


def _tile_call(name, main, main_args, main_specs, x_shape, sides=()):
    kern = functools.partial(
        _joint_kernel, main=main, n_main=len(main_args),
        sides=tuple((fn, len(args), len(shapes)) for fn, args, _, shapes, _ in sides))
    outs = pl.pallas_call(
        kern,
        out_shape=[jax.ShapeDtypeStruct(x_shape, F32)] + [s for side in sides for s in side[3]],
        grid=(N_TILES, N_GROUPS),
        in_specs=list(main_specs) + [s for side in sides for s in side[2]],
        out_specs=[_tile_spec(D_MODEL)] + [s for side in sides for s in side[4]],
        compiler_params=_params(("arbitrary", "arbitrary")),
        name=name,
    )(*main_args, *[a for side in sides for a in side[1]])
    return outs


def _ffn(x, mods, norm_g, wg, wu, wd, *, layer, half, row0, per_group, sides=()):
    kern = functools.partial(_ffn_kernel, mod0=6 * half, gain_row=2 * half)
    specs = [
        _tile_spec(D_MODEL),
        _mod_spec(row0, per_group, group_axis=1),
        _resident((None, 3, D_MODEL), lambda t, g: (layer, 0, 0)),
    ] + _ffn_weight_specs(half)
    return _tile_call("ffn", kern, (x, mods, norm_g, wg, wu, wd), specs, x.shape, sides)


def _to_bf16_side(arrays):
    flat = [a.reshape(-1, a.shape[-1]) for a in arrays]
    specs = [pl.BlockSpec((a.shape[0] // N_STEPS, a.shape[1]), lambda t, g: (_step(t, g), 0))
             for a in flat]

    def jobs(*refs):
        def cast(src, dst):
            def run():
                dst[...] = src[...].astype(BF16)
            return _one_stage(run)
        return [cast(src, dst) for src, dst in zip(refs[:len(flat)], refs[len(flat):])]

    return jobs, flat, specs, [jax.ShapeDtypeStruct(a.shape, BF16) for a in flat], specs


LANES = 128
GROUP_COLS = 256


def _pool_column(a, lane0, seq_len):
    t = a.shape[0]
    pos = lax.broadcasted_iota(jnp.int32, a.shape, 0) & (seq_len - 1)
    lane = lax.broadcasted_iota(jnp.int32, (1, LANES), 1) + lane0
    half = jnp.left_shift(1, lane >> 6)
    max_half = 1 << ((lane0 + LANES - 1) // HEAD_DIM)
    right = a
    left = a
    k = 1
    while 2 * k <= max_half:
        grow = half >= 2 * k
        right = right + jnp.where(pos < jnp.where(grow, seq_len - k, 0),
                                  pltpu.roll(right, t - k, 0), 0.0)
        left = left + jnp.where(pos >= jnp.where(grow, k, seq_len),
                                pltpu.roll(left, k, 0), 0.0)
        k *= 2
    window = right + jnp.where(pos >= 1, pltpu.roll(left, 1, 0), 0.0)
    cnt = jnp.minimum(pos + half, seq_len) - jnp.maximum(pos - half, 0)
    return window / cnt.astype(F32) - a


def _pool_mixer(a, seq_len):
    cols = [_pool_column(a[:, c:c + LANES], c, seq_len) for c in range(0, a.shape[1], LANES)]
    return jnp.concatenate(cols, axis=-1)


def _group_rms(z, ones_ref):
    outs = []
    for c in range(z.shape[1] // GROUP_COLS):
        zc = z[:, c * GROUP_COLS:(c + 1) * GROUP_COLS]
        ms = _dot((zc * zc).astype(BF16), ones_ref[...]) * (1.0 / HEAD_DIM)
        outs.append(zc * lax.rsqrt(ms + EPS))
    return outs[0] if len(outs) == 1 else jnp.concatenate(outs, axis=-1)


def _mixer_in_kernel(x_ref, mod_ref, gain_ref, w_in_ref, pool_w_ref, pool_s_ref,
                     qg_ref, kg_ref, vng_ref, sgw_ref, sgb_ref, og_ref, ones_ref,
                     *refs, seq_len, n_alias):
    oa_ref, q_ref, k_ref, v_ref, oc_ref, *kv_refs = refs[n_alias:]
    m = mod_ref[...]
    og = og_ref[...]
    gc = og[:, POOL_WIDTH + NA_WIDTH:]
    lane_grp = lax.broadcasted_iota(jnp.int32, (CHUNK, SG_WIDTH), 1) >> 6
    sgw = sgw_ref[...]
    sgb = sgb_ref[...]
    splits = (0, POOL_WIDTH, POOL_WIDTH + NA_WIDTH, POOL_WIDTH + 2 * NA_WIDTH,
              POOL_WIDTH + 3 * NA_WIDTH, POOL_WIDTH + 3 * NA_WIDTH + SG_WIDTH, w_in_ref.shape[1])

    def pool_out(rows, za):
        pooled = _pool_mixer(za, seq_len)
        a_out = _dot(pooled.astype(BF16), pool_w_ref[...]) * pool_s_ref[...]
        oa_ref[rows, :] = _rms(a_out, og[:, 0:POOL_WIDTH]).astype(BF16)

    tiles = [slice(r, r + MIX_TM) for r in range(0, GROUP, MIX_TM)]
    projected = []
    for rows in tiles:
        h = _rms_mod(x_ref[rows, :], gain_ref[1:2, :], m[3:4], m[4:5]).astype(BF16)
        projected.append([_dot(h, w_in_ref[:, a:b]) for a, b in zip(splits[:-1], splits[1:])])

    for rows, (za, zq, zk, zv, zu, zsv) in zip(tiles, projected):
        if seq_len <= MIX_TM:
            pool_out(rows, za)

        q_ref[rows, :] = (_group_rms(zq, ones_ref) * (qg_ref[...] * QUERY_SCALE)).astype(BF16)
        k = _group_rms(zk, ones_ref) * kg_ref[...]
        k_ref[rows, :] = k.astype(BF16)
        v_ref[rows, :] = zv.astype(BF16)
        for val, ref in zip((k, zv), kv_refs):
            for s in range(MIX_TM // SEQ):
                for hd in range(NA_HEADS):
                    ref[rows.start // SEQ + s, hd] = val[s * SEQ:(s + 1) * SEQ,
                                                         hd * HEAD_DIM:(hd + 1) * HEAD_DIM]

        u = jax.nn.gelu(zu)
        vg = (_group_rms(jax.nn.gelu(zsv), ones_ref) * vng_ref[...]).astype(BF16)
        for r0 in range(0, MIX_TM, CHUNK):
            full = _dot(sgw, vg[r0:r0 + CHUNK, :])
            sp = full[0:CHUNK]
            for gi in range(1, 4):
                sp = jnp.where(lane_grp == gi, full[gi * CHUNK:(gi + 1) * CHUNK], sp)
            c_out = u[r0:r0 + CHUNK, :] * (sp + sgb)
            oc_ref[rows.start + r0:rows.start + r0 + CHUNK, :] = _rms(c_out, gc).astype(BF16)

    if seq_len > MIX_TM:
        pool_out(slice(0, GROUP), jnp.concatenate([z[0] for z in projected], axis=0))


def _mixer_in(x, mods, norm_g, w_in, pool_w, pool_s, qg, kg, vng, sgw, sgb, og, ones,
              *, layer, row0, per_group, seq_len, emit_kv, prev_kv=()):
    kern = functools.partial(_mixer_in_kernel, seq_len=seq_len, n_alias=len(prev_kv))
    tok = lambda w, dt: jax.ShapeDtypeStruct((N_GROUPS, GROUP, w), dt)
    tok_spec = lambda w: pl.BlockSpec((None, GROUP, w), lambda g: (g, 0, 0))
    lay = lambda *shape: _resident((None,) + shape, lambda g: (layer,) + (0,) * len(shape))
    n_seq = GROUP // SEQ
    kv_shape = jax.ShapeDtypeStruct((N_GROUPS * n_seq, DEPTH, NA_HEADS, SEQ, HEAD_DIM), F32)
    kv_spec = pl.BlockSpec((n_seq, None, NA_HEADS, SEQ, HEAD_DIM), lambda g: (g, layer, 0, 0, 0))
    kv_shapes = (kv_shape, kv_shape) if emit_kv else ()
    kv_specs = (kv_spec, kv_spec) if emit_kv else ()
    n_in = 13
    aliases = {n_in + i: 5 + i for i in range(len(prev_kv))}
    return pl.pallas_call(
        kern,
        out_shape=(tok(POOL_WIDTH, BF16), tok(NA_WIDTH, BF16), tok(NA_WIDTH, BF16),
                   tok(NA_WIDTH, BF16), tok(SG_WIDTH, BF16)) + kv_shapes,
        grid=(N_GROUPS,),
        in_specs=[
            tok_spec(D_MODEL),
            _mod_spec(row0, per_group),
            lay(3, D_MODEL),
            _resident(w_in.shape, lambda g: (0, 0)),
            lay(POOL_WIDTH, POOL_WIDTH),
            lay(1, POOL_WIDTH),
            lay(1, NA_WIDTH),
            lay(1, NA_WIDTH),
            lay(1, SG_WIDTH),
            lay(4 * CHUNK, CHUNK),
            lay(CHUNK, SG_WIDTH),
            lay(1, D_MODEL),
            _resident((GROUP_COLS, GROUP_COLS), lambda g: (0, 0)),
        ] + [pl.BlockSpec(memory_space=pl.ANY)] * len(prev_kv),
        out_specs=(tok_spec(POOL_WIDTH), tok_spec(NA_WIDTH), tok_spec(NA_WIDTH),
                   tok_spec(NA_WIDTH), tok_spec(SG_WIDTH)) + kv_specs,
        input_output_aliases=aliases,
        compiler_params=_params(("arbitrary",)),
        name="mixer_in",
    )(x, mods, norm_g, w_in, pool_w, pool_s, qg, kg, vng, sgw, sgb, og, ones, *prev_kv)


def _first_head(shape):
    return lax.broadcasted_iota(jnp.int32, shape, 1) < HEAD_DIM


def _head_queries(q):
    first = _first_head(q.shape)
    zero = jnp.zeros_like(q)
    return jnp.where(first, q, zero), jnp.where(first, zero, q)


def _head_values(v):
    first = _first_head(v.shape)
    one = jnp.ones_like(v)
    return jnp.where(first, v, one), jnp.where(first, one, v)


def _softmax_pv(scores, values, values_t=()):
    mx = scores[0].max(axis=-1, keepdims=True)
    for s in scores[1:]:
        mx = jnp.maximum(mx, s.max(axis=-1, keepdims=True))
    out = 0.0
    for i, s in enumerate(scores):
        e = jnp.exp2(s - mx).astype(BF16)
        out = out + (_dot(e, values[i]) if i < len(values)
                     else _dot_nt(e, values_t[i - len(values)]))
    return out


def _merge_heads(o0, o1):
    first = _first_head(o0.shape)
    num = jnp.where(first, o0, o1)
    den = pltpu.roll(jnp.where(first, o1, o0), HEAD_DIM, 1)
    return (num / den).astype(BF16)


def _ctx_attn_jobs(q_ref, k_ref, v_ref, o_ref):
    def job(rows, lanes):
        def start():
            k = k_ref[rows, lanes]
            q0, q1 = _head_queries(q_ref[rows, lanes])
            s0, s1 = _dot_nt(q0, k), _dot_nt(q1, k)

            def finish():
                v0, v1 = _head_values(v_ref[rows, lanes])
                o_ref[rows, lanes] = _merge_heads(_softmax_pv([s0], [v0]),
                                                  _softmax_pv([s1], [v1]))
            return finish
        return start

    return [job(slice(s, s + SEQ), slice(p, p + PAIR_W))
            for p in range(0, q_ref.shape[1], PAIR_W) for s in range(0, GROUP, SEQ)]


def _side_spec():
    return pl.BlockSpec((None, GROUP, SIDE_PAIRS * PAIR_W), lambda t, g: (g, 0, t))


def _ctx_attn_side(q, k, v):
    spec = _side_spec()
    return (_ctx_attn_jobs, (q, k, v), [spec, spec, spec],
            [jax.ShapeDtypeStruct(q.shape, BF16)], [spec])


NA_QROWS = 4


def _na_plan():
    blocks, tables, offsets, width = [], [], [], 0
    for r0 in range(0, GRID_ROWS, NA_QROWS):
        rs = range(r0, r0 + NA_QROWS)
        starts = [min(max(r - NA_ROWS // 2, 0), GRID_ROWS - NA_ROWS) for r in rs]
        nk = max(starts) + NA_ROWS - min(starts)
        nk += nk % 2
        k0 = min(min(starts), GRID_ROWS - nk)
        table = (nk, tuple((k0 - r + NA_ROWS - 1, s - k0, s - k0 + NA_ROWS)
                           for r, s in zip(rs, starts)))
        assert all(d >= 0 and d + nk <= N_RPB_ROWS + 1 for d, _, _ in table[1])
        if table not in tables:
            tables.append(table)
            offsets.append(width)
            width += nk * GRID_W
        blocks.append((r0, k0, nk, offsets[tables.index(table)]))
    return tuple(blocks), tuple(tables), width


def _bias_kernel(rexp_ref, o_ref, *, tables):
    shape = (GRID_W, 16 * GRID_W)
    cq = lax.broadcasted_iota(jnp.int32, shape, 0)
    ck = lax.broadcasted_iota(jnp.int32, shape, 1) & (GRID_W - 1)
    dc = jnp.clip(ck - cq, -(NA_COLS - 1), NA_COLS - 1) + (NA_COLS - 1)
    col_start = jnp.clip(cq - NA_COLS // 2, 0, GRID_W - NA_COLS)
    ok = (ck >= col_start) & (ck < col_start + NA_COLS)
    table = jnp.zeros(shape, F32)
    for j in range(N_RPB_COLS):
        table = jnp.where(dc == j, jnp.broadcast_to(rexp_ref[j:j + 1, :], shape), table)
    table = jnp.where(ok, table * LOG2E, MASKED)
    col0 = 0
    for nk, rows in tables:
        width = nk * GRID_W
        krow = lax.broadcasted_iota(jnp.int32, (GRID_W, width), 1) >> 6
        for rl, (d, lo, hi) in enumerate(rows):
            slab = table[:, d * GRID_W:d * GRID_W + width]
            if lo > 0 or hi < nk:
                slab = jnp.where((krow >= lo) & (krow < hi), slab, MASKED)
            o_ref[rl * GRID_W:(rl + 1) * GRID_W, col0:col0 + width] = slab
        col0 += width


def _bias_tables(rexp, tables, width):
    return pl.pallas_call(
        functools.partial(_bias_kernel, tables=tables),
        out_shape=jax.ShapeDtypeStruct((DEPTH, NA_HEADS, NA_QROWS * GRID_W, width), F32),
        grid=(DEPTH, NA_HEADS),
        in_specs=[pl.BlockSpec((None, None, N_RPB_COLS + 1, 16 * GRID_W), lambda l, h: (l, h, 0, 0))],
        out_specs=pl.BlockSpec((None, None, NA_QROWS * GRID_W, width), lambda l, h: (l, h, 0, 0)),
        compiler_params=_params(("arbitrary", "arbitrary")),
        name="na_bias",
    )(rexp)


def _na_attn_jobs(q_ref, k_ref, v_ref, ck_ref, cv_ref, bias_ref, o_ref, *, blocks):
    pair_cache = {}

    def cached_pair(ref, p):
        key = (id(ref), p)
        if key not in pair_cache:
            pair_cache[key] = jnp.concatenate([ref[2 * p], ref[2 * p + 1]], axis=0).astype(BF16)
        return pair_cache[key]

    def cached_values(p):
        if ("values", p) not in pair_cache:
            vt = cached_pair(cv_ref, p)
            first = lax.broadcasted_iota(jnp.int32, vt.shape, 0) < HEAD_DIM
            one = jnp.ones_like(vt)
            pair_cache["values", p] = (jnp.where(first, vt, one), jnp.where(first, one, vt))
        return pair_cache["values", p]

    def job(p, r0, k0, nk, col0):
        lanes = slice(p * PAIR_W, (p + 1) * PAIR_W)
        h0 = 2 * p
        q_rows = slice(r0 * GRID_W, (r0 + NA_QROWS) * GRID_W)
        k_rows = slice(k0 * GRID_W, (k0 + nk) * GRID_W)

        def start():
            kc_t = cached_pair(ck_ref, p)
            k_loc = k_ref[k_rows, lanes]
            scores = []
            for e, qe in enumerate(_head_queries(q_ref[q_rows, lanes])):
                s_loc = _dot_nt(qe, k_loc) + bias_ref[h0 + e, :, col0:col0 + nk * GRID_W]
                scores.append([s_loc, _dot(qe, kc_t)])

            def finish():
                vc_t = cached_values(p)
                v_loc = _head_values(v_ref[k_rows, lanes])
                o_ref[q_rows, lanes] = _merge_heads(
                    *[_softmax_pv(scores[e], [v_loc[e]], [vc_t[e]]) for e in range(2)])
            return finish
        return start

    return [job(p, *blk) for p in range(q_ref.shape[1] // PAIR_W) for blk in blocks]


def _na_attn_side(q, k, v, cache_kt, cache_vt, bias, blocks, *, layer):
    spec = _side_spec()
    n_heads = 2 * SIDE_PAIRS
    cspec = pl.BlockSpec((None, None, n_heads, HEAD_DIM, PAST_LEN),
                         lambda t, g: (g, layer, t, 0, 0))
    bspec = _resident((None, n_heads) + bias.shape[2:], lambda t, g: (layer, t, 0, 0))
    return (functools.partial(_na_attn_jobs, blocks=blocks),
            (q, k, v, cache_kt, cache_vt, bias), [spec, spec, spec, cspec, cspec, bspec],
            [jax.ShapeDtypeStruct(q.shape, BF16)], [spec])


def _mixer_out_ffn_kernel(x_ref, oa_ref, ob_ref, oc_ref, mod_ref, gain_ref, og_ref, wo_ref,
                          wg_ref, wu_ref, wd_ref, o_ref, *, side_jobs=()):
    m = mod_ref[...]
    gb = og_ref[:, POOL_WIDTH:POOL_WIDTH + NA_WIDTH]
    c0 = POOL_WIDTH
    c1 = POOL_WIDTH + NA_WIDTH
    ob = _rms(ob_ref[...].astype(F32), gb).astype(BF16)
    o = (_dot(oa_ref[...], wo_ref[0:c0, :]) + _dot(ob, wo_ref[c0:c1, :])
         + _dot(oc_ref[...], wo_ref[c1:, :]))
    x = x_ref[...] + m[5:6] * o
    o_ref[...] = _ffn_tile(x, gain_ref[2:3, :], m[6:7], m[7:8], m[8:9], wg_ref, wu_ref, wd_ref,
                           side_jobs)


def _mixer_out_ffn(x, oa, ob, oc, mods, norm_g, og, wo, wg, wu, wd, *, layer, row0, per_group,
                   sides=()):
    specs = [
        _tile_spec(D_MODEL), _tile_spec(POOL_WIDTH), _tile_spec(NA_WIDTH), _tile_spec(SG_WIDTH),
        _mod_spec(row0, per_group, group_axis=1),
        _resident((None, 3, D_MODEL), lambda t, g: (layer, 0, 0)),
        _resident((None, 1, D_MODEL), lambda t, g: (layer, 0, 0)),
        _resident((D_MODEL, D_MODEL), lambda t, g: (0, 0)),
    ] + _ffn_weight_specs(1)
    return _tile_call("mixer_out_ffn", _mixer_out_ffn_kernel,
                      (x, oa, ob, oc, mods, norm_g, og, wo, wg, wu, wd), specs, x.shape, sides)


def _block_diag(w):
    out = jnp.zeros((DEPTH, POOL_WIDTH, POOL_WIDTH), w.dtype)
    for g in range(4):
        lo = g * HEAD_DIM
        out = out.at[:, lo:lo + HEAD_DIM, lo:lo + HEAD_DIM].set(w[:, g])
    return out


def kernel(x_prompt, x_sample, cache_k, cache_v, c, c_ctx, ada_w, ada_b, norm_g, ffn_w_gate,
           ffn_w_up, ffn_w_down, w_in, pool_w, pool_scale, q_norm_g, k_norm_g, na_rpb,
           sg_vnorm_g, sg_w, sg_b, out_norm_g, w_out):
    dec_batch = x_sample.shape[0]

    cvec = jnp.concatenate([c_ctx[None], c, jnp.zeros((16 - 1 - dec_batch, D_MODEL), F32)])
    all_mods = _adaln(cvec, ada_w, ada_b)

    big = (ffn_w_gate, ffn_w_up, ffn_w_down, w_in, w_out)
    wg, wu, wd, w_in_b, wo = (w[0].astype(BF16) for w in big)
    pool_wb = _block_diag(pool_w).astype(BF16)
    pool_s = pool_scale.reshape(DEPTH, 1, POOL_WIDTH)
    qg = jnp.tile(q_norm_g, (1, NA_HEADS)).reshape(DEPTH, 1, NA_WIDTH)
    kg = jnp.tile(k_norm_g, (1, NA_HEADS)).reshape(DEPTH, 1, NA_WIDTH)
    vng = sg_vnorm_g.reshape(DEPTH, 1, SG_WIDTH)
    sgw = sg_w.reshape(DEPTH, 4 * CHUNK, CHUNK).astype(BF16)
    sgb = jnp.repeat(jnp.swapaxes(sg_b, 1, 2), HEAD_DIM, axis=-1)
    og = out_norm_g.reshape(DEPTH, 1, D_MODEL)
    lane = jnp.arange(GROUP_COLS) // HEAD_DIM
    ones = (lane[:, None] == lane[None, :]).astype(BF16)
    rexp = jnp.repeat(jnp.swapaxes(na_rpb, 2, 3), GRID_W, axis=-1)
    rexp = jnp.pad(rexp, ((0, 0), (0, 0), (0, 1), (0, GRID_W)))
    na_blocks, na_tables, na_width = _na_plan()
    cache_kt = jnp.swapaxes(cache_k, 3, 4)
    cache_vt = jnp.swapaxes(cache_v, 3, 4)

    xp = x_prompt.reshape(N_GROUPS, GROUP, D_MODEL)
    xs = x_sample.reshape(N_GROUPS, GROUP, D_MODEL)
    ctx = dict(row0=0, per_group=0)
    lat = dict(row0=1, per_group=1)
    new_kv = ()
    bias = _bias_tables(rexp, na_tables, na_width)
    for l in range(DEPTH):
        last = l == DEPTH - 1
        mods = all_mods[l]
        xp, = _ffn(xp, mods, norm_g, wg, wu, wd, layer=l, half=0, **ctx)
        mix = (mods, norm_g, w_in_b, pool_wb, pool_s, qg, kg, vng, sgw, sgb, og, ones)
        oa_p, q, k, v, oc_p, *new_kv = _mixer_in(xp, *mix, layer=l, seq_len=SEQ, emit_kv=True,
                                                 prev_kv=new_kv, **ctx)
        xs, ob_p = _ffn(xs, mods, norm_g, wg, wu, wd, layer=l, half=0, **lat,
                        sides=[_ctx_attn_side(q, k, v)])
        oa_s, q, k, v, oc_s = _mixer_in(xs, *mix, layer=l, seq_len=GROUP, emit_kv=False, **lat)
        xp, ob_s = _mixer_out_ffn(
            xp, oa_p, ob_p, oc_p, mods, norm_g, og, wo, wg, wu, wd, layer=l, **ctx,
            sides=[_na_attn_side(q, k, v, cache_kt, cache_vt, bias, na_blocks, layer=l)])
        sides = [] if last else [_to_bf16_side([w[l + 1] for w in big])]
        xs, *cast = _mixer_out_ffn(xs, oa_s, ob_s, oc_s, mods, norm_g, og, wo, wg, wu, wd,
                                   layer=l, **lat, sides=sides)
        if not last:
            wg, wu, wd, w_in_b, wo = (a.reshape(w.shape[1:]) for a, w in zip(cast, big))

    return (xp.reshape(x_prompt.shape), xs.reshape(x_sample.shape)) + tuple(new_kv)
```

```python
import functools

import jax
import jax.numpy as jnp
from jax import lax
from jax.experimental import pallas as pl
from jax.experimental.pallas import tpu as pltpu

F32 = jnp.float32
BF16 = jnp.bfloat16

D_MODEL = 1024
D_FF = 2816
N_MOD = 9
DEPTH = 2
GROUP = 1024
N_GROUPS = 8
SEQ = 256
GRID_W = 64
GRID_ROWS = GROUP // GRID_W
POOL_WIDTH = 256
NA_WIDTH = 512
SG_WIDTH = 256
HEAD_DIM = 64
NA_HEADS = 8
HEAD_PAIRS = NA_HEADS // 2
PAIR_W = 2 * HEAD_DIM
NA_ROWS = 8
NA_COLS = 16
N_RPB_ROWS = 2 * NA_ROWS - 1
N_RPB_COLS = 2 * NA_COLS - 1
CHUNK = 128
PAST_LEN = 256
EPS = 1e-6
MASKED = -1e30
LOG2E = 1.4426950408889634
QUERY_SCALE = HEAD_DIM ** -0.5 * LOG2E

FFN_TM = 512
FFN_FC = 256
MIX_TM = 512
VMEM_LIMIT = 56 * 1024 * 1024


def _dot(a, b):
    return jnp.dot(a, b, preferred_element_type=F32)


def _dot_nt(a, b):
    return lax.dot_general(a, b, (((1,), (1,)), ((), ())), preferred_element_type=F32)


def _rms(x, gain):
    ms = jnp.mean(x * x, axis=-1, keepdims=True)
    return x * lax.rsqrt(ms + EPS) * gain


def _rms_mod(x, gain, shift, scale):
    ms = jnp.mean(x * x, axis=-1, keepdims=True)
    return x * lax.rsqrt(ms + EPS) * (gain * (1.0 + scale)) + shift


def _params(sem):
    return pltpu.CompilerParams(dimension_semantics=sem, vmem_limit_bytes=VMEM_LIMIT)


def _resident(shape, index_map):
    return pl.BlockSpec(shape, index_map, pipeline_mode=pl.Buffered(1))


ADALN_TN = 1024


def _adaln_kernel(c_ref, w_ref, b_ref, o_ref):
    s = jax.nn.silu(c_ref[...]).astype(BF16)
    o_ref[...] = _dot(s, w_ref[...].astype(BF16)) + b_ref[...]


def _adaln(cvec, ada_w, ada_b):
    rows = cvec.shape[0]
    out = pl.pallas_call(
        _adaln_kernel,
        out_shape=jax.ShapeDtypeStruct((DEPTH, rows, N_MOD * D_MODEL), F32),
        grid=(DEPTH, N_MOD * D_MODEL // ADALN_TN),
        in_specs=[
            pl.BlockSpec((rows, D_MODEL), lambda l, n: (0, 0)),
            pl.BlockSpec((None, D_MODEL, ADALN_TN), lambda l, n: (l, 0, n)),
            pl.BlockSpec((None, 1, ADALN_TN), lambda l, n: (l, 0, n)),
        ],
        out_specs=pl.BlockSpec((None, rows, ADALN_TN), lambda l, n: (l, 0, n)),
        compiler_params=_params(("arbitrary", "arbitrary")),
        name="adaln",
    )(cvec, ada_w, ada_b.reshape(DEPTH, 1, N_MOD * D_MODEL))
    return out.reshape(DEPTH, rows, N_MOD, D_MODEL)


def _ffn_tile(x, gain, shift, scale, gate, wg_ref, wu_ref, wd_ref, side_jobs=()):
    h = _rms_mod(x, gain, shift, scale).astype(BF16)
    acc = jnp.zeros(x.shape, F32)
    n_chunks = D_FF // FFN_FC
    per_chunk = -(-len(side_jobs) // n_chunks)
    for c in range(n_chunks):
        finishers = [job() for job in side_jobs[c * per_chunk:(c + 1) * per_chunk]]
        lo = c * FFN_FC
        g = _dot(h, wg_ref[:, lo:lo + FFN_FC])
        u = _dot(h, wu_ref[:, lo:lo + FFN_FC])
        a = (g * jax.nn.sigmoid(g) * u).astype(BF16)
        acc = acc + _dot(a, wd_ref[lo:lo + FFN_FC, :])
        for finish in finishers:
            finish()
    return x + 0.5 * gate * acc


def _ffn_kernel(x_ref, mod_ref, gain_ref, wg_ref, wu_ref, wd_ref, o_ref, *, mod0, gain_row,
                side_jobs=()):
    m = mod_ref[...]
    o_ref[...] = _ffn_tile(x_ref[...], gain_ref[gain_row:gain_row + 1, :],
                           m[mod0:mod0 + 1], m[mod0 + 1:mod0 + 2], m[mod0 + 2:mod0 + 3],
                           wg_ref, wu_ref, wd_ref, side_jobs)


def _mod_spec(row0, per_group, group_axis=0):
    return pl.BlockSpec((None, N_MOD, D_MODEL),
                        lambda *idx: (row0 + idx[group_axis] * per_group, 0, 0))


def _ffn_weight_specs(half):
    return [
        _resident((None, D_MODEL, D_FF), lambda *_: (half, 0, 0)),
        _resident((None, D_MODEL, D_FF), lambda *_: (half, 0, 0)),
        _resident((None, D_FF, D_MODEL), lambda *_: (half, 0, 0)),
    ]


N_TILES = GROUP // FFN_TM
N_STEPS = N_TILES * N_GROUPS
SIDE_PAIRS = HEAD_PAIRS // N_TILES


def _tile_spec(width):
    return pl.BlockSpec((None, FFN_TM, width), lambda t, g: (g, t, 0))


def _step(t, g):
    return t * N_GROUPS + g


def _one_stage(fn):
    def start():
        fn()
        return lambda: None
    return start


def _joint_kernel(*refs, main, n_main, sides):
    outs = refs[n_main + sum(n_in for _, n_in, _ in sides):]
    jobs = []
    i, o = n_main, 1
    for jobs_fn, n_in, n_out in sides:
        jobs += jobs_fn(*refs[i:i + n_in], *outs[o:o + n_out])
        i, o = i + n_in, o + n_out
    main(*refs[:n_main], outs[0], side_jobs=jobs)


def _tile_call(name, main, main_args, main_specs, x_shape, sides=()):
    kern = functools.partial(
        _joint_kernel, main=main, n_main=len(main_args),
        sides=tuple((fn, len(args), len(shapes)) for fn, args, _, shapes, _ in sides))
    outs = pl.pallas_call(
        kern,
        out_shape=[jax.ShapeDtypeStruct(x_shape, F32)] + [s for side in sides for s in side[3]],
        grid=(N_TILES, N_GROUPS),
        in_specs=list(main_specs) + [s for side in sides for s in side[2]],
        out_specs=[_tile_spec(D_MODEL)] + [s for side in sides for s in side[4]],
        compiler_params=_params(("arbitrary", "arbitrary")),
        name=name,
    )(*main_args, *[a for side in sides for a in side[1]])
    return outs


def _ffn(x, mods, norm_g, wg, wu, wd, *, layer, half, row0, per_group, sides=()):
    kern = functools.partial(_ffn_kernel, mod0=6 * half, gain_row=2 * half)
    specs = [
        _tile_spec(D_MODEL),
        _mod_spec(row0, per_group, group_axis=1),
        _resident((None, 3, D_MODEL), lambda t, g: (layer, 0, 0)),
    ] + _ffn_weight_specs(half)
    return _tile_call("ffn", kern, (x, mods, norm_g, wg, wu, wd), specs, x.shape, sides)


def _to_bf16_side(arrays, layer):
    flat = [a.reshape(-1, a.shape[-1]) for a in arrays]
    rows = [a.shape[0] // DEPTH for a in flat]
    in_specs = [pl.BlockSpec((r // N_STEPS, a.shape[1]),
                             lambda t, g: (layer * N_STEPS + _step(t, g), 0))
                for a, r in zip(flat, rows)]
    out_specs = [pl.BlockSpec((r // N_STEPS, a.shape[1]), lambda t, g: (_step(t, g), 0))
                 for a, r in zip(flat, rows)]
    out_shapes = [jax.ShapeDtypeStruct((r, a.shape[1]), BF16) for a, r in zip(flat, rows)]

    def jobs(*refs):
        def cast(src, dst):
            def run():
                dst[...] = src[...].astype(BF16)
            return _one_stage(run)
        return [cast(src, dst) for src, dst in zip(refs[:len(flat)], refs[len(flat):])]

    return jobs, flat, in_specs, out_shapes, out_specs


LANES = 128
GROUP_COLS = 256


def _pool_column(a, lane0, seq_len):
    t = a.shape[0]
    pos = lax.broadcasted_iota(jnp.int32, a.shape, 0) & (seq_len - 1)
    lane = lax.broadcasted_iota(jnp.int32, (1, LANES), 1) + lane0
    half = jnp.left_shift(1, lane >> 6)
    max_half = 1 << ((lane0 + LANES - 1) // HEAD_DIM)
    right = a
    left = a
    k = 1
    while 2 * k <= max_half:
        grow = half >= 2 * k
        right = right + jnp.where(pos < jnp.where(grow, seq_len - k, 0),
                                  pltpu.roll(right, t - k, 0), 0.0)
        left = left + jnp.where(pos >= jnp.where(grow, k, seq_len),
                                pltpu.roll(left, k, 0), 0.0)
        k *= 2
    window = right + jnp.where(pos >= 1, pltpu.roll(left, 1, 0), 0.0)
    cnt = jnp.minimum(pos + half, seq_len) - jnp.maximum(pos - half, 0)
    return window / cnt.astype(F32) - a


def _pool_mixer(a, seq_len):
    cols = [_pool_column(a[:, c:c + LANES], c, seq_len) for c in range(0, a.shape[1], LANES)]
    return jnp.concatenate(cols, axis=-1)


def _group_rms(z, ones_ref):
    outs = []
    for c in range(z.shape[1] // GROUP_COLS):
        zc = z[:, c * GROUP_COLS:(c + 1) * GROUP_COLS]
        ms = _dot((zc * zc).astype(BF16), ones_ref[...]) * (1.0 / HEAD_DIM)
        outs.append(zc * lax.rsqrt(ms + EPS))
    return outs[0] if len(outs) == 1 else jnp.concatenate(outs, axis=-1)


def _mixer_in_kernel(x_ref, mod_ref, gain_ref, w_in_ref, pool_w_ref, pool_s_ref,
                     qg_ref, kg_ref, vng_ref, sgw_ref, sgb_ref, og_ref, ones_ref,
                     *refs, seq_len, n_alias):
    oa_ref, q_ref, k_ref, v_ref, oc_ref, *kv_refs = refs[n_alias:]
    m = mod_ref[...]
    og = og_ref[...]
    gc = og[:, POOL_WIDTH + NA_WIDTH:]
    lane_grp = lax.broadcasted_iota(jnp.int32, (CHUNK, SG_WIDTH), 1) >> 6
    sgw = sgw_ref[...]
    sgb = sgb_ref[...]
    splits = (0, POOL_WIDTH, POOL_WIDTH + NA_WIDTH, POOL_WIDTH + 2 * NA_WIDTH,
              POOL_WIDTH + 3 * NA_WIDTH, POOL_WIDTH + 3 * NA_WIDTH + SG_WIDTH, w_in_ref.shape[1])

    def pool_out(rows, za):
        pooled = _pool_mixer(za, seq_len)
        a_out = _dot(pooled.astype(BF16), pool_w_ref[...]) * pool_s_ref[...]
        oa_ref[rows, :] = _rms(a_out, og[:, 0:POOL_WIDTH]).astype(BF16)

    tiles = [slice(r, r + MIX_TM) for r in range(0, GROUP, MIX_TM)]
    projected = []
    for rows in tiles:
        h = _rms_mod(x_ref[rows, :], gain_ref[1:2, :], m[3:4], m[4:5]).astype(BF16)
        projected.append([_dot(h, w_in_ref[:, a:b]) for a, b in zip(splits[:-1], splits[1:])])

    for rows, (za, zq, zk, zv, zu, zsv) in zip(tiles, projected):
        if seq_len <= MIX_TM:
            pool_out(rows, za)

        q_ref[rows, :] = (_group_rms(zq, ones_ref) * (qg_ref[...] * QUERY_SCALE)).astype(BF16)
        k = _group_rms(zk, ones_ref) * kg_ref[...]
        k_ref[rows, :] = k.astype(BF16)
        v_ref[rows, :] = zv.astype(BF16)
        for val, ref in zip((k, zv), kv_refs):
            for s in range(MIX_TM // SEQ):
                for hd in range(NA_HEADS):
                    ref[rows.start // SEQ + s, hd] = val[s * SEQ:(s + 1) * SEQ,
                                                         hd * HEAD_DIM:(hd + 1) * HEAD_DIM]

        u = jax.nn.gelu(zu)
        vg = (_group_rms(jax.nn.gelu(zsv), ones_ref) * vng_ref[...]).astype(BF16)
        for r0 in range(0, MIX_TM, CHUNK):
            full = _dot(sgw, vg[r0:r0 + CHUNK, :])
            sp = full[0:CHUNK]
            for gi in range(1, 4):
                sp = jnp.where(lane_grp == gi, full[gi * CHUNK:(gi + 1) * CHUNK], sp)
            c_out = u[r0:r0 + CHUNK, :] * (sp + sgb)
            oc_ref[rows.start + r0:rows.start + r0 + CHUNK, :] = _rms(c_out, gc).astype(BF16)

    if seq_len > MIX_TM:
        pool_out(slice(0, GROUP), jnp.concatenate([z[0] for z in projected], axis=0))


def _mixer_in(x, mods, norm_g, w_in, pool_w, pool_s, qg, kg, vng, sgw, sgb, og, ones,
              *, layer, row0, per_group, seq_len, emit_kv, prev_kv=()):
    kern = functools.partial(_mixer_in_kernel, seq_len=seq_len, n_alias=len(prev_kv))
    tok = lambda w, dt: jax.ShapeDtypeStruct((N_GROUPS, GROUP, w), dt)
    tok_spec = lambda w: pl.BlockSpec((None, GROUP, w), lambda g: (g, 0, 0))
    lay = lambda *shape: _resident((None,) + shape, lambda g: (layer,) + (0,) * len(shape))
    n_seq = GROUP // SEQ
    kv_shape = jax.ShapeDtypeStruct((N_GROUPS * n_seq, DEPTH, NA_HEADS, SEQ, HEAD_DIM), F32)
    kv_spec = pl.BlockSpec((n_seq, None, NA_HEADS, SEQ, HEAD_DIM), lambda g: (g, layer, 0, 0, 0))
    kv_shapes = (kv_shape, kv_shape) if emit_kv else ()
    kv_specs = (kv_spec, kv_spec) if emit_kv else ()
    n_in = 13
    aliases = {n_in + i: 5 + i for i in range(len(prev_kv))}
    return pl.pallas_call(
        kern,
        out_shape=(tok(POOL_WIDTH, BF16), tok(NA_WIDTH, BF16), tok(NA_WIDTH, BF16),
                   tok(NA_WIDTH, BF16), tok(SG_WIDTH, BF16)) + kv_shapes,
        grid=(N_GROUPS,),
        in_specs=[
            tok_spec(D_MODEL),
            _mod_spec(row0, per_group),
            lay(3, D_MODEL),
            _resident(w_in.shape, lambda g: (0, 0)),
            lay(POOL_WIDTH, POOL_WIDTH),
            lay(1, POOL_WIDTH),
            lay(1, NA_WIDTH),
            lay(1, NA_WIDTH),
            lay(1, SG_WIDTH),
            lay(4 * CHUNK, CHUNK),
            lay(CHUNK, SG_WIDTH),
            lay(1, D_MODEL),
            _resident((GROUP_COLS, GROUP_COLS), lambda g: (0, 0)),
        ] + [pl.BlockSpec(memory_space=pl.ANY)] * len(prev_kv),
        out_specs=(tok_spec(POOL_WIDTH), tok_spec(NA_WIDTH), tok_spec(NA_WIDTH),
                   tok_spec(NA_WIDTH), tok_spec(SG_WIDTH)) + kv_specs,
        input_output_aliases=aliases,
        compiler_params=_params(("arbitrary",)),
        name="mixer_in",
    )(x, mods, norm_g, w_in, pool_w, pool_s, qg, kg, vng, sgw, sgb, og, ones, *prev_kv)


def _first_head(shape):
    return lax.broadcasted_iota(jnp.int32, shape, 1) < HEAD_DIM


def _head_queries(q):
    first = _first_head(q.shape)
    zero = jnp.zeros_like(q)
    return jnp.where(first, q, zero), jnp.where(first, zero, q)


def _head_values(v):
    first = _first_head(v.shape)
    one = jnp.ones_like(v)
    return jnp.where(first, v, one), jnp.where(first, one, v)


def _softmax_pv(scores, values, values_t=()):
    mx = scores[0].max(axis=-1, keepdims=True)
    for s in scores[1:]:
        mx = jnp.maximum(mx, s.max(axis=-1, keepdims=True))
    out = 0.0
    for i, s in enumerate(scores):
        e = jnp.exp2(s - mx).astype(BF16)
        out = out + (_dot(e, values[i]) if i < len(values)
                     else _dot_nt(e, values_t[i - len(values)]))
    return out


def _merge_heads(o0, o1):
    first = _first_head(o0.shape)
    num = jnp.where(first, o0, o1)
    den = pltpu.roll(jnp.where(first, o1, o0), HEAD_DIM, 1)
    return (num / den).astype(BF16)


def _ctx_attn_jobs(q_ref, k_ref, v_ref, o_ref):
    def job(rows, lanes):
        def start():
            k = k_ref[rows, lanes]
            q0, q1 = _head_queries(q_ref[rows, lanes])
            s0, s1 = _dot_nt(q0, k), _dot_nt(q1, k)

            def finish():
                v0, v1 = _head_values(v_ref[rows, lanes])
                o_ref[rows, lanes] = _merge_heads(_softmax_pv([s0], [v0]),
                                                  _softmax_pv([s1], [v1]))
            return finish
        return start

    return [job(slice(s, s + SEQ), slice(p, p + PAIR_W))
            for p in range(0, q_ref.shape[1], PAIR_W) for s in range(0, GROUP, SEQ)]


def _side_spec():
    return pl.BlockSpec((None, GROUP, SIDE_PAIRS * PAIR_W), lambda t, g: (g, 0, t))


def _ctx_attn_side(q, k, v):
    spec = _side_spec()
    return (_ctx_attn_jobs, (q, k, v), [spec, spec, spec],
            [jax.ShapeDtypeStruct(q.shape, BF16)], [spec])


NA_QROWS = 4


def _na_plan():
    blocks, tables, offsets, width = [], [], [], 0
    for r0 in range(0, GRID_ROWS, NA_QROWS):
        rs = range(r0, r0 + NA_QROWS)
        starts = [min(max(r - NA_ROWS // 2, 0), GRID_ROWS - NA_ROWS) for r in rs]
        nk = max(starts) + NA_ROWS - min(starts)
        nk += nk % 2
        k0 = min(min(starts), GRID_ROWS - nk)
        table = (nk, tuple((k0 - r + NA_ROWS - 1, s - k0, s - k0 + NA_ROWS)
                           for r, s in zip(rs, starts)))
        assert all(d >= 0 and d + nk <= N_RPB_ROWS + 1 for d, _, _ in table[1])
        if table not in tables:
            tables.append(table)
            offsets.append(width)
            width += nk * GRID_W
        blocks.append((r0, k0, nk, offsets[tables.index(table)]))
    return tuple(blocks), tuple(tables), width


def _bias_kernel(rexp_ref, o_ref, *, tables):
    shape = (GRID_W, 16 * GRID_W)
    cq = lax.broadcasted_iota(jnp.int32, shape, 0)
    ck = lax.broadcasted_iota(jnp.int32, shape, 1) & (GRID_W - 1)
    dc = jnp.clip(ck - cq, -(NA_COLS - 1), NA_COLS - 1) + (NA_COLS - 1)
    col_start = jnp.clip(cq - NA_COLS // 2, 0, GRID_W - NA_COLS)
    ok = (ck >= col_start) & (ck < col_start + NA_COLS)
    table = jnp.zeros(shape, F32)
    for j in range(N_RPB_COLS):
        table = jnp.where(dc == j, jnp.broadcast_to(rexp_ref[j:j + 1, :], shape), table)
    table = jnp.where(ok, table * LOG2E, MASKED)
    col0 = 0
    for nk, rows in tables:
        width = nk * GRID_W
        krow = lax.broadcasted_iota(jnp.int32, (GRID_W, width), 1) >> 6
        for rl, (d, lo, hi) in enumerate(rows):
            slab = table[:, d * GRID_W:d * GRID_W + width]
            if lo > 0 or hi < nk:
                slab = jnp.where((krow >= lo) & (krow < hi), slab, MASKED)
            o_ref[rl * GRID_W:(rl + 1) * GRID_W, col0:col0 + width] = slab
        col0 += width


def _bias_tables(rexp, tables, width):
    return pl.pallas_call(
        functools.partial(_bias_kernel, tables=tables),
        out_shape=jax.ShapeDtypeStruct((DEPTH, NA_HEADS, NA_QROWS * GRID_W, width), F32),
        grid=(DEPTH, NA_HEADS),
        in_specs=[pl.BlockSpec((None, None, N_RPB_COLS + 1, 16 * GRID_W), lambda l, h: (l, h, 0, 0))],
        out_specs=pl.BlockSpec((None, None, NA_QROWS * GRID_W, width), lambda l, h: (l, h, 0, 0)),
        compiler_params=_params(("arbitrary", "arbitrary")),
        name="na_bias",
    )(rexp)


def _na_attn_jobs(q_ref, k_ref, v_ref, ck_ref, cv_ref, bias_ref, o_ref, *, blocks):
    pair_cache = {}

    def cached_pair(ref, p):
        key = (id(ref), p)
        if key not in pair_cache:
            pair_cache[key] = jnp.concatenate([ref[2 * p], ref[2 * p + 1]], axis=0).astype(BF16)
        return pair_cache[key]

    def cached_values(p):
        if ("values", p) not in pair_cache:
            vt = cached_pair(cv_ref, p)
            first = lax.broadcasted_iota(jnp.int32, vt.shape, 0) < HEAD_DIM
            one = jnp.ones_like(vt)
            pair_cache["values", p] = (jnp.where(first, vt, one), jnp.where(first, one, vt))
        return pair_cache["values", p]

    def job(p, r0, k0, nk, col0):
        lanes = slice(p * PAIR_W, (p + 1) * PAIR_W)
        h0 = 2 * p
        q_rows = slice(r0 * GRID_W, (r0 + NA_QROWS) * GRID_W)
        k_rows = slice(k0 * GRID_W, (k0 + nk) * GRID_W)

        def start():
            kc_t = cached_pair(ck_ref, p)
            k_loc = k_ref[k_rows, lanes]
            scores = []
            for e, qe in enumerate(_head_queries(q_ref[q_rows, lanes])):
                s_loc = _dot_nt(qe, k_loc) + bias_ref[h0 + e, :, col0:col0 + nk * GRID_W]
                scores.append([s_loc, _dot(qe, kc_t)])

            def finish():
                vc_t = cached_values(p)
                v_loc = _head_values(v_ref[k_rows, lanes])
                o_ref[q_rows, lanes] = _merge_heads(
                    *[_softmax_pv(scores[e], [v_loc[e]], [vc_t[e]]) for e in range(2)])
            return finish
        return start

    return [job(p, *blk) for p in range(q_ref.shape[1] // PAIR_W) for blk in blocks]


def _na_attn_side(q, k, v, cache_kt, cache_vt, bias, blocks, *, layer):
    spec = _side_spec()
    n_heads = 2 * SIDE_PAIRS
    cspec = pl.BlockSpec((None, None, n_heads, HEAD_DIM, PAST_LEN),
                         lambda t, g: (g, layer, t, 0, 0))
    bspec = _resident((None, n_heads) + bias.shape[2:], lambda t, g: (layer, t, 0, 0))
    return (functools.partial(_na_attn_jobs, blocks=blocks),
            (q, k, v, cache_kt, cache_vt, bias), [spec, spec, spec, cspec, cspec, bspec],
            [jax.ShapeDtypeStruct(q.shape, BF16)], [spec])


def _mixer_out_ffn_kernel(x_ref, oa_ref, ob_ref, oc_ref, mod_ref, gain_ref, og_ref, wo_ref,
                          wg_ref, wu_ref, wd_ref, o_ref, *, side_jobs=()):
    m = mod_ref[...]
    gb = og_ref[:, POOL_WIDTH:POOL_WIDTH + NA_WIDTH]
    c0 = POOL_WIDTH
    c1 = POOL_WIDTH + NA_WIDTH
    ob = _rms(ob_ref[...].astype(F32), gb).astype(BF16)
    o = (_dot(oa_ref[...], wo_ref[0:c0, :]) + _dot(ob, wo_ref[c0:c1, :])
         + _dot(oc_ref[...], wo_ref[c1:, :]))
    x = x_ref[...] + m[5:6] * o
    o_ref[...] = _ffn_tile(x, gain_ref[2:3, :], m[6:7], m[7:8], m[8:9], wg_ref, wu_ref, wd_ref,
                           side_jobs)


def _mixer_out_ffn(x, oa, ob, oc, mods, norm_g, og, wo, wg, wu, wd, *, layer, row0, per_group,
                   sides=()):
    specs = [
        _tile_spec(D_MODEL), _tile_spec(POOL_WIDTH), _tile_spec(NA_WIDTH), _tile_spec(SG_WIDTH),
        _mod_spec(row0, per_group, group_axis=1),
        _resident((None, 3, D_MODEL), lambda t, g: (layer, 0, 0)),
        _resident((None, 1, D_MODEL), lambda t, g: (layer, 0, 0)),
        _resident((D_MODEL, D_MODEL), lambda t, g: (0, 0)),
    ] + _ffn_weight_specs(1)
    return _tile_call("mixer_out_ffn", _mixer_out_ffn_kernel,
                      (x, oa, ob, oc, mods, norm_g, og, wo, wg, wu, wd), specs, x.shape, sides)


def _block_diag(w):
    out = jnp.zeros((DEPTH, POOL_WIDTH, POOL_WIDTH), w.dtype)
    for g in range(4):
        lo = g * HEAD_DIM
        out = out.at[:, lo:lo + HEAD_DIM, lo:lo + HEAD_DIM].set(w[:, g])
    return out


def kernel(x_prompt, x_sample, cache_k, cache_v, c, c_ctx, ada_w, ada_b, norm_g, ffn_w_gate,
           ffn_w_up, ffn_w_down, w_in, pool_w, pool_scale, q_norm_g, k_norm_g, na_rpb,
           sg_vnorm_g, sg_w, sg_b, out_norm_g, w_out):
    dec_batch = x_sample.shape[0]

    cvec = jnp.concatenate([c_ctx[None], c, jnp.zeros((16 - 1 - dec_batch, D_MODEL), F32)])
    all_mods = _adaln(cvec, ada_w, ada_b)

    big = (ffn_w_gate, ffn_w_up, ffn_w_down, w_in, w_out)
    wg, wu, wd, w_in_b, wo = (w[0].astype(BF16) for w in big)
    pool_wb = _block_diag(pool_w).astype(BF16)
    pool_s = pool_scale.reshape(DEPTH, 1, POOL_WIDTH)
    qg = jnp.tile(q_norm_g, (1, NA_HEADS)).reshape(DEPTH, 1, NA_WIDTH)
    kg = jnp.tile(k_norm_g, (1, NA_HEADS)).reshape(DEPTH, 1, NA_WIDTH)
    vng = sg_vnorm_g.reshape(DEPTH, 1, SG_WIDTH)
    sgw = sg_w.reshape(DEPTH, 4 * CHUNK, CHUNK).astype(BF16)
    sgb = jnp.repeat(jnp.swapaxes(sg_b, 1, 2), HEAD_DIM, axis=-1)
    og = out_norm_g.reshape(DEPTH, 1, D_MODEL)
    lane = jnp.arange(GROUP_COLS) // HEAD_DIM
    ones = (lane[:, None] == lane[None, :]).astype(BF16)
    rexp = jnp.repeat(jnp.swapaxes(na_rpb, 2, 3), GRID_W, axis=-1)
    rexp = jnp.pad(rexp, ((0, 0), (0, 0), (0, 1), (0, GRID_W)))
    na_blocks, na_tables, na_width = _na_plan()
    cache_kt = jnp.swapaxes(cache_k, 3, 4)
    cache_vt = jnp.swapaxes(cache_v, 3, 4)

    xp = x_prompt.reshape(N_GROUPS, GROUP, D_MODEL)
    xs = x_sample.reshape(N_GROUPS, GROUP, D_MODEL)
    ctx = dict(row0=0, per_group=0)
    lat = dict(row0=1, per_group=1)
    new_kv = ()
    bias = _bias_tables(rexp, na_tables, na_width)
    for l in range(DEPTH):
        last = l == DEPTH - 1
        mods = all_mods[l]
        xp, = _ffn(xp, mods, norm_g, wg, wu, wd, layer=l, half=0, **ctx)
        mix = (mods, norm_g, w_in_b, pool_wb, pool_s, qg, kg, vng, sgw, sgb, og, ones)
        oa_p, q, k, v, oc_p, *new_kv = _mixer_in(xp, *mix, layer=l, seq_len=SEQ, emit_kv=True,
                                                 prev_kv=new_kv, **ctx)
        xs, ob_p = _ffn(xs, mods, norm_g, wg, wu, wd, layer=l, half=0, **lat,
                        sides=[_ctx_attn_side(q, k, v)])
        oa_s, q, k, v, oc_s = _mixer_in(xs, *mix, layer=l, seq_len=GROUP, emit_kv=False, **lat)
        xp, ob_s = _mixer_out_ffn(
            xp, oa_p, ob_p, oc_p, mods, norm_g, og, wo, wg, wu, wd, layer=l, **ctx,
            sides=[_na_attn_side(q, k, v, cache_kt, cache_vt, bias, na_blocks, layer=l)])
        sides = [] if last else [_to_bf16_side(big, l + 1)]
        xs, *cast = _mixer_out_ffn(xs, oa_s, ob_s, oc_s, mods, norm_g, og, wo, wg, wu, wd,
                                   layer=l, **lat, sides=sides)
        if not last:
            wg, wu, wd, w_in_b, wo = (a.reshape(w.shape[1:]) for a, w in zip(cast, big))

    return (xp.reshape(x_prompt.shape), xs.reshape(x_sample.shape)) + tuple(new_kv)
```

```python
import functools

import jax
import jax.numpy as jnp
from jax import lax
from jax.experimental import pallas as pl
from jax.experimental.pallas import tpu as pltpu

F32 = jnp.float32
BF16 = jnp.bfloat16

D_MODEL = 1024
D_FF = 2816
N_MOD = 9
DEPTH = 2
GROUP = 1024
N_GROUPS = 8
SEQ = 256
GRID_W = 64
GRID_ROWS = GROUP // GRID_W
POOL_WIDTH = 256
NA_WIDTH = 512
SG_WIDTH = 256
HEAD_DIM = 64
NA_HEADS = 8
HEAD_PAIRS = NA_HEADS // 2
PAIR_W = 2 * HEAD_DIM
NA_ROWS = 8
NA_COLS = 16
N_RPB_ROWS = 2 * NA_ROWS - 1
N_RPB_COLS = 2 * NA_COLS - 1
CHUNK = 128
PAST_LEN = 256
EPS = 1e-6
MASKED = -1e30
LOG2E = 1.4426950408889634
QUERY_SCALE = HEAD_DIM ** -0.5 * LOG2E

FFN_TM = 512
FFN_FC = 256
MIX_TM = 512
VMEM_LIMIT = 56 * 1024 * 1024


def _dot(a, b):
    return jnp.dot(a, b, preferred_element_type=F32)


def _dot_nt(a, b):
    return lax.dot_general(a, b, (((1,), (1,)), ((), ())), preferred_element_type=F32)


def _rms(x, gain):
    ms = jnp.mean(x * x, axis=-1, keepdims=True)
    return x * lax.rsqrt(ms + EPS) * gain


def _rms_mod(x, gain, shift, scale):
    ms = jnp.mean(x * x, axis=-1, keepdims=True)
    return x * lax.rsqrt(ms + EPS) * (gain * (1.0 + scale)) + shift


def _params(sem):
    return pltpu.CompilerParams(dimension_semantics=sem, vmem_limit_bytes=VMEM_LIMIT)


def _resident(shape, index_map):
    return pl.BlockSpec(shape, index_map, pipeline_mode=pl.Buffered(1))


ADALN_TN = 1024


def _adaln_kernel(c_ref, w_ref, b_ref, o_ref):
    s = jax.nn.silu(c_ref[...]).astype(BF16)
    o_ref[...] = _dot(s, w_ref[...].astype(BF16)) + b_ref[...]


ADALN_TILES = N_MOD * D_MODEL // ADALN_TN
PREP_STEPS = DEPTH * ADALN_TILES
PREP_PARTS = DEPTH * NA_HEADS


def _prep_kernel(c_ref, aw_ref, ab_ref, rexp_ref, *refs, tables, n_cast):
    mods_ref, bias_ref = refs[n_cast:n_cast + 2]
    _adaln_kernel(c_ref, aw_ref, ab_ref, mods_ref)
    _bias_kernel(rexp_ref, bias_ref, tables=tables)
    for src, dst in zip(refs[:n_cast], refs[n_cast + 2:]):
        dst[...] = src[...].astype(BF16)


def _prepare(cvec, ada_w, ada_b, rexp, tables, width, weights):
    rows = cvec.shape[0]
    flat = [w.reshape(-1, w.shape[-1]) for w in weights]
    w_rows = [a.shape[0] // DEPTH for a in flat]
    part = lambda s: jnp.minimum(s, PREP_PARTS - 1)
    tile = lambda s: (s // ADALN_TILES, 0, s % ADALN_TILES)
    head = lambda s: (part(s) // NA_HEADS, part(s) % NA_HEADS, 0, 0)
    slab = [pl.BlockSpec((r // PREP_PARTS, a.shape[1]), lambda s: (part(s), 0))
            for a, r in zip(flat, w_rows)]
    outs = pl.pallas_call(
        functools.partial(_prep_kernel, tables=tables, n_cast=len(flat)),
        out_shape=[jax.ShapeDtypeStruct((DEPTH, rows, N_MOD * D_MODEL), F32),
                   jax.ShapeDtypeStruct((DEPTH, NA_HEADS, NA_QROWS * GRID_W, width), F32)]
        + [jax.ShapeDtypeStruct((r, a.shape[1]), BF16) for a, r in zip(flat, w_rows)],
        grid=(PREP_STEPS,),
        in_specs=[
            pl.BlockSpec((rows, D_MODEL), lambda s: (0, 0)),
            pl.BlockSpec((None, D_MODEL, ADALN_TN), tile),
            pl.BlockSpec((None, 1, ADALN_TN), tile),
            pl.BlockSpec((None, None, N_RPB_COLS + 1, 16 * GRID_W), head),
        ] + slab,
        out_specs=[
            pl.BlockSpec((None, rows, ADALN_TN), tile),
            pl.BlockSpec((None, None, NA_QROWS * GRID_W, width), head),
        ] + slab,
        compiler_params=_params(("arbitrary",)),
        name="prepare",
    )(cvec, ada_w, ada_b.reshape(DEPTH, 1, N_MOD * D_MODEL), rexp, *flat)
    mods = outs[0].reshape(DEPTH, rows, N_MOD, D_MODEL)
    return mods, outs[1], [a.reshape(w.shape[1:]) for a, w in zip(outs[2:], weights)]


def _ffn_tile(x, gain, shift, scale, gate, wg_ref, wu_ref, wd_ref, side_jobs=()):
    h = _rms_mod(x, gain, shift, scale).astype(BF16)
    acc = jnp.zeros(x.shape, F32)
    n_chunks = D_FF // FFN_FC
    per_chunk = -(-len(side_jobs) // n_chunks)
    for c in range(n_chunks):
        finishers = [job() for job in side_jobs[c * per_chunk:(c + 1) * per_chunk]]
        lo = c * FFN_FC
        g = _dot(h, wg_ref[:, lo:lo + FFN_FC])
        u = _dot(h, wu_ref[:, lo:lo + FFN_FC])
        a = (g * jax.nn.sigmoid(g) * u).astype(BF16)
        acc = acc + _dot(a, wd_ref[lo:lo + FFN_FC, :])
        for finish in finishers:
            finish()
    return x + 0.5 * gate * acc


def _ffn_kernel(x_ref, mod_ref, gain_ref, wg_ref, wu_ref, wd_ref, o_ref, *, mod0, gain_row,
                side_jobs=()):
    m = mod_ref[...]
    o_ref[...] = _ffn_tile(x_ref[...], gain_ref[gain_row:gain_row + 1, :],
                           m[mod0:mod0 + 1], m[mod0 + 1:mod0 + 2], m[mod0 + 2:mod0 + 3],
                           wg_ref, wu_ref, wd_ref, side_jobs)


def _mod_spec(row0, per_group, group_axis=0):
    return pl.BlockSpec((None, N_MOD, D_MODEL),
                        lambda *idx: (row0 + idx[group_axis] * per_group, 0, 0))


def _ffn_weight_specs(half):
    return [
        _resident((None, D_MODEL, D_FF), lambda *_: (half, 0, 0)),
        _resident((None, D_MODEL, D_FF), lambda *_: (half, 0, 0)),
        _resident((None, D_FF, D_MODEL), lambda *_: (half, 0, 0)),
    ]


N_TILES = GROUP // FFN_TM
N_STEPS = N_TILES * N_GROUPS
SIDE_PAIRS = HEAD_PAIRS // N_TILES


def _tile_spec(width):
    return pl.BlockSpec((None, FFN_TM, width), lambda t, g: (g, t, 0))


def _step(t, g):
    return t * N_GROUPS + g


def _one_stage(fn):
    def start():
        fn()
        return lambda: None
    return start


def _joint_kernel(*refs, main, n_main, sides):
    outs = refs[n_main + sum(n_in for _, n_in, _ in sides):]
    jobs = []
    i, o = n_main, 1
    for jobs_fn, n_in, n_out in sides:
        jobs += jobs_fn(*refs[i:i + n_in], *outs[o:o + n_out])
        i, o = i + n_in, o + n_out
    main(*refs[:n_main], outs[0], side_jobs=jobs)


def _tile_call(name, main, main_args, main_specs, x_shape, sides=()):
    kern = functools.partial(
        _joint_kernel, main=main, n_main=len(main_args),
        sides=tuple((fn, len(args), len(shapes)) for fn, args, _, shapes, _ in sides))
    outs = pl.pallas_call(
        kern,
        out_shape=[jax.ShapeDtypeStruct(x_shape, F32)] + [s for side in sides for s in side[3]],
        grid=(N_TILES, N_GROUPS),
        in_specs=list(main_specs) + [s for side in sides for s in side[2]],
        out_specs=[_tile_spec(D_MODEL)] + [s for side in sides for s in side[4]],
        compiler_params=_params(("arbitrary", "arbitrary")),
        name=name,
    )(*main_args, *[a for side in sides for a in side[1]])
    return outs


def _ffn(x, mods, norm_g, wg, wu, wd, *, layer, half, row0, per_group, sides=()):
    kern = functools.partial(_ffn_kernel, mod0=6 * half, gain_row=2 * half)
    specs = [
        _tile_spec(D_MODEL),
        _mod_spec(row0, per_group, group_axis=1),
        _resident((None, 3, D_MODEL), lambda t, g: (layer, 0, 0)),
    ] + _ffn_weight_specs(half)
    return _tile_call("ffn", kern, (x, mods, norm_g, wg, wu, wd), specs, x.shape, sides)


def _to_bf16_side(arrays, layer):
    flat = [a.reshape(-1, a.shape[-1]) for a in arrays]
    rows = [a.shape[0] // DEPTH for a in flat]
    in_specs = [pl.BlockSpec((r // N_STEPS, a.shape[1]),
                             lambda t, g: (layer * N_STEPS + _step(t, g), 0))
                for a, r in zip(flat, rows)]
    out_specs = [pl.BlockSpec((r // N_STEPS, a.shape[1]), lambda t, g: (_step(t, g), 0))
                 for a, r in zip(flat, rows)]
    out_shapes = [jax.ShapeDtypeStruct((r, a.shape[1]), BF16) for a, r in zip(flat, rows)]

    def jobs(*refs):
        def cast(src, dst):
            def run():
                dst[...] = src[...].astype(BF16)
            return _one_stage(run)
        return [cast(src, dst) for src, dst in zip(refs[:len(flat)], refs[len(flat):])]

    return jobs, flat, in_specs, out_shapes, out_specs


LANES = 128
GROUP_COLS = 256


def _pool_column(a, lane0, seq_len):
    t = a.shape[0]
    pos = lax.broadcasted_iota(jnp.int32, a.shape, 0) & (seq_len - 1)
    lane = lax.broadcasted_iota(jnp.int32, (1, LANES), 1) + lane0
    half = jnp.left_shift(1, lane >> 6)
    max_half = 1 << ((lane0 + LANES - 1) // HEAD_DIM)
    right = a
    left = a
    k = 1
    while 2 * k <= max_half:
        grow = half >= 2 * k
        right = right + jnp.where(pos < jnp.where(grow, seq_len - k, 0),
                                  pltpu.roll(right, t - k, 0), 0.0)
        left = left + jnp.where(pos >= jnp.where(grow, k, seq_len),
                                pltpu.roll(left, k, 0), 0.0)
        k *= 2
    window = right + jnp.where(pos >= 1, pltpu.roll(left, 1, 0), 0.0)
    cnt = jnp.minimum(pos + half, seq_len) - jnp.maximum(pos - half, 0)
    return window / cnt.astype(F32) - a


def _pool_mixer(a, seq_len):
    cols = [_pool_column(a[:, c:c + LANES], c, seq_len) for c in range(0, a.shape[1], LANES)]
    return jnp.concatenate(cols, axis=-1)


def _group_rms(z, ones_ref):
    outs = []
    for c in range(z.shape[1] // GROUP_COLS):
        zc = z[:, c * GROUP_COLS:(c + 1) * GROUP_COLS]
        ms = _dot((zc * zc).astype(BF16), ones_ref[...]) * (1.0 / HEAD_DIM)
        outs.append(zc * lax.rsqrt(ms + EPS))
    return outs[0] if len(outs) == 1 else jnp.concatenate(outs, axis=-1)


def _mixer_in_kernel(x_ref, mod_ref, gain_ref, w_in_ref, pool_w_ref, pool_s_ref,
                     qg_ref, kg_ref, vng_ref, sgw_ref, sgb_ref, og_ref, ones_ref,
                     *refs, seq_len, n_alias):
    oa_ref, q_ref, k_ref, v_ref, oc_ref, *kv_refs = refs[n_alias:]
    m = mod_ref[...]
    og = og_ref[...]
    gc = og[:, POOL_WIDTH + NA_WIDTH:]
    lane_grp = lax.broadcasted_iota(jnp.int32, (CHUNK, SG_WIDTH), 1) >> 6
    sgw = sgw_ref[...]
    sgb = sgb_ref[...]
    splits = (0, POOL_WIDTH, POOL_WIDTH + NA_WIDTH, POOL_WIDTH + 2 * NA_WIDTH,
              POOL_WIDTH + 3 * NA_WIDTH, POOL_WIDTH + 3 * NA_WIDTH + SG_WIDTH, w_in_ref.shape[1])

    def pool_out(rows, za):
        pooled = _pool_mixer(za, seq_len)
        a_out = _dot(pooled.astype(BF16), pool_w_ref[...]) * pool_s_ref[...]
        oa_ref[rows, :] = _rms(a_out, og[:, 0:POOL_WIDTH]).astype(BF16)

    tiles = [slice(r, r + MIX_TM) for r in range(0, GROUP, MIX_TM)]
    projected = []
    for rows in tiles:
        h = _rms_mod(x_ref[rows, :], gain_ref[1:2, :], m[3:4], m[4:5]).astype(BF16)
        projected.append([_dot(h, w_in_ref[:, a:b]) for a, b in zip(splits[:-1], splits[1:])])

    for rows, (za, zq, zk, zv, zu, zsv) in zip(tiles, projected):
        if seq_len <= MIX_TM:
            pool_out(rows, za)

        q_ref[rows, :] = (_group_rms(zq, ones_ref) * (qg_ref[...] * QUERY_SCALE)).astype(BF16)
        k = _group_rms(zk, ones_ref) * kg_ref[...]
        k_ref[rows, :] = k.astype(BF16)
        v_ref[rows, :] = zv.astype(BF16)
        for val, ref in zip((k, zv), kv_refs):
            for s in range(MIX_TM // SEQ):
                for p in range(HEAD_PAIRS):
                    pair_t = val[s * SEQ:(s + 1) * SEQ, p * PAIR_W:(p + 1) * PAIR_W].T
                    ref[rows.start // SEQ + s, 2 * p] = pair_t[:HEAD_DIM]
                    ref[rows.start // SEQ + s, 2 * p + 1] = pair_t[HEAD_DIM:]

        u = jax.nn.gelu(zu)
        vg = (_group_rms(jax.nn.gelu(zsv), ones_ref) * vng_ref[...]).astype(BF16)
        for r0 in range(0, MIX_TM, CHUNK):
            full = _dot(sgw, vg[r0:r0 + CHUNK, :])
            sp = full[0:CHUNK]
            for gi in range(1, 4):
                sp = jnp.where(lane_grp == gi, full[gi * CHUNK:(gi + 1) * CHUNK], sp)
            c_out = u[r0:r0 + CHUNK, :] * (sp + sgb)
            oc_ref[rows.start + r0:rows.start + r0 + CHUNK, :] = _rms(c_out, gc).astype(BF16)

    if seq_len > MIX_TM:
        pool_out(slice(0, GROUP), jnp.concatenate([z[0] for z in projected], axis=0))


def _mixer_in(x, mods, norm_g, w_in, pool_w, pool_s, qg, kg, vng, sgw, sgb, og, ones,
              *, layer, row0, per_group, seq_len, emit_kv, prev_kv=()):
    kern = functools.partial(_mixer_in_kernel, seq_len=seq_len, n_alias=len(prev_kv))
    tok = lambda w, dt: jax.ShapeDtypeStruct((N_GROUPS, GROUP, w), dt)
    tok_spec = lambda w: pl.BlockSpec((None, GROUP, w), lambda g: (g, 0, 0))
    lay = lambda *shape: _resident((None,) + shape, lambda g: (layer,) + (0,) * len(shape))
    n_seq = GROUP // SEQ
    kv_shape = jax.ShapeDtypeStruct((N_GROUPS * n_seq, DEPTH, NA_HEADS, HEAD_DIM, SEQ), F32)
    kv_spec = pl.BlockSpec((n_seq, None, NA_HEADS, HEAD_DIM, SEQ), lambda g: (g, layer, 0, 0, 0))
    kv_shapes = (kv_shape, kv_shape) if emit_kv else ()
    kv_specs = (kv_spec, kv_spec) if emit_kv else ()
    n_in = 13
    aliases = {n_in + i: 5 + i for i in range(len(prev_kv))}
    return pl.pallas_call(
        kern,
        out_shape=(tok(POOL_WIDTH, BF16), tok(NA_WIDTH, BF16), tok(NA_WIDTH, BF16),
                   tok(NA_WIDTH, BF16), tok(SG_WIDTH, BF16)) + kv_shapes,
        grid=(N_GROUPS,),
        in_specs=[
            tok_spec(D_MODEL),
            _mod_spec(row0, per_group),
            lay(3, D_MODEL),
            _resident(w_in.shape, lambda g: (0, 0)),
            lay(POOL_WIDTH, POOL_WIDTH),
            lay(1, POOL_WIDTH),
            lay(1, NA_WIDTH),
            lay(1, NA_WIDTH),
            lay(1, SG_WIDTH),
            lay(4 * CHUNK, CHUNK),
            lay(CHUNK, SG_WIDTH),
            lay(1, D_MODEL),
            _resident((GROUP_COLS, GROUP_COLS), lambda g: (0, 0)),
        ] + [pl.BlockSpec(memory_space=pl.ANY)] * len(prev_kv),
        out_specs=(tok_spec(POOL_WIDTH), tok_spec(NA_WIDTH), tok_spec(NA_WIDTH),
                   tok_spec(NA_WIDTH), tok_spec(SG_WIDTH)) + kv_specs,
        input_output_aliases=aliases,
        compiler_params=_params(("arbitrary",)),
        name="mixer_in",
    )(x, mods, norm_g, w_in, pool_w, pool_s, qg, kg, vng, sgw, sgb, og, ones, *prev_kv)


def _first_head(shape):
    return lax.broadcasted_iota(jnp.int32, shape, 1) < HEAD_DIM


def _head_queries(q):
    first = _first_head(q.shape)
    zero = jnp.zeros_like(q)
    return jnp.where(first, q, zero), jnp.where(first, zero, q)


def _head_values(v):
    first = _first_head(v.shape)
    one = jnp.ones_like(v)
    return jnp.where(first, v, one), jnp.where(first, one, v)


def _softmax_pv(scores, values, values_t=()):
    mx = scores[0].max(axis=-1, keepdims=True)
    for s in scores[1:]:
        mx = jnp.maximum(mx, s.max(axis=-1, keepdims=True))
    out = 0.0
    for i, s in enumerate(scores):
        e = jnp.exp2(s - mx).astype(BF16)
        out = out + (_dot(e, values[i]) if i < len(values)
                     else _dot_nt(e, values_t[i - len(values)]))
    return out


def _merge_heads(o0, o1):
    first = _first_head(o0.shape)
    num = jnp.where(first, o0, o1)
    den = pltpu.roll(jnp.where(first, o1, o0), HEAD_DIM, 1)
    return (num / den).astype(BF16)


def _ctx_attn_jobs(q_ref, k_ref, v_ref, o_ref):
    def job(rows, lanes):
        def start():
            k = k_ref[rows, lanes]
            q0, q1 = _head_queries(q_ref[rows, lanes])
            s0, s1 = _dot_nt(q0, k), _dot_nt(q1, k)

            def finish():
                v0, v1 = _head_values(v_ref[rows, lanes])
                o_ref[rows, lanes] = _merge_heads(_softmax_pv([s0], [v0]),
                                                  _softmax_pv([s1], [v1]))
            return finish
        return start

    return [job(slice(s, s + SEQ), slice(p, p + PAIR_W))
            for p in range(0, q_ref.shape[1], PAIR_W) for s in range(0, GROUP, SEQ)]


def _side_spec():
    return pl.BlockSpec((None, GROUP, SIDE_PAIRS * PAIR_W), lambda t, g: (g, 0, t))


def _ctx_attn_side(q, k, v):
    spec = _side_spec()
    return (_ctx_attn_jobs, (q, k, v), [spec, spec, spec],
            [jax.ShapeDtypeStruct(q.shape, BF16)], [spec])


NA_QROWS = 4


def _na_plan():
    blocks, tables, offsets, width = [], [], [], 0
    for r0 in range(0, GRID_ROWS, NA_QROWS):
        rs = range(r0, r0 + NA_QROWS)
        starts = [min(max(r - NA_ROWS // 2, 0), GRID_ROWS - NA_ROWS) for r in rs]
        nk = max(starts) + NA_ROWS - min(starts)
        nk += nk % 2
        k0 = min(min(starts), GRID_ROWS - nk)
        table = (nk, tuple((k0 - r + NA_ROWS - 1, s - k0, s - k0 + NA_ROWS)
                           for r, s in zip(rs, starts)))
        assert all(d >= 0 and d + nk <= N_RPB_ROWS + 1 for d, _, _ in table[1])
        if table not in tables:
            tables.append(table)
            offsets.append(width)
            width += nk * GRID_W
        blocks.append((r0, k0, nk, offsets[tables.index(table)]))
    return tuple(blocks), tuple(tables), width


def _bias_kernel(rexp_ref, o_ref, *, tables):
    shape = (GRID_W, 16 * GRID_W)
    cq = lax.broadcasted_iota(jnp.int32, shape, 0)
    ck = lax.broadcasted_iota(jnp.int32, shape, 1) & (GRID_W - 1)
    dc = jnp.clip(ck - cq, -(NA_COLS - 1), NA_COLS - 1) + (NA_COLS - 1)
    col_start = jnp.clip(cq - NA_COLS // 2, 0, GRID_W - NA_COLS)
    ok = (ck >= col_start) & (ck < col_start + NA_COLS)
    table = jnp.zeros(shape, F32)
    for j in range(N_RPB_COLS):
        table = jnp.where(dc == j, jnp.broadcast_to(rexp_ref[j:j + 1, :], shape), table)
    table = jnp.where(ok, table * LOG2E, MASKED)
    col0 = 0
    for nk, rows in tables:
        width = nk * GRID_W
        krow = lax.broadcasted_iota(jnp.int32, (GRID_W, width), 1) >> 6
        for rl, (d, lo, hi) in enumerate(rows):
            slab = table[:, d * GRID_W:d * GRID_W + width]
            if lo > 0 or hi < nk:
                slab = jnp.where((krow >= lo) & (krow < hi), slab, MASKED)
            o_ref[rl * GRID_W:(rl + 1) * GRID_W, col0:col0 + width] = slab
        col0 += width


def _na_attn_jobs(q_ref, k_ref, v_ref, ck_ref, cv_ref, bias_ref, o_ref, *, blocks):
    pair_cache = {}

    def cached_pair(ref, p):
        key = (id(ref), p)
        if key not in pair_cache:
            pair_cache[key] = jnp.concatenate([ref[2 * p], ref[2 * p + 1]], axis=0).astype(BF16)
        return pair_cache[key]

    def cached_values(p):
        if ("values", p) not in pair_cache:
            vt = cached_pair(cv_ref, p)
            first = lax.broadcasted_iota(jnp.int32, vt.shape, 0) < HEAD_DIM
            one = jnp.ones_like(vt)
            pair_cache["values", p] = (jnp.where(first, vt, one), jnp.where(first, one, vt))
        return pair_cache["values", p]

    def job(p, r0, k0, nk, col0):
        lanes = slice(p * PAIR_W, (p + 1) * PAIR_W)
        h0 = 2 * p
        q_rows = slice(r0 * GRID_W, (r0 + NA_QROWS) * GRID_W)
        k_rows = slice(k0 * GRID_W, (k0 + nk) * GRID_W)

        def start():
            kc_t = cached_pair(ck_ref, p)
            k_loc = k_ref[k_rows, lanes]
            scores = []
            for e, qe in enumerate(_head_queries(q_ref[q_rows, lanes])):
                s_loc = _dot_nt(qe, k_loc) + bias_ref[h0 + e, :, col0:col0 + nk * GRID_W]
                scores.append([s_loc, _dot(qe, kc_t)])

            def finish():
                vc_t = cached_values(p)
                v_loc = _head_values(v_ref[k_rows, lanes])
                o_ref[q_rows, lanes] = _merge_heads(
                    *[_softmax_pv(scores[e], [v_loc[e]], [vc_t[e]]) for e in range(2)])
            return finish
        return start

    return [job(p, *blk) for p in range(q_ref.shape[1] // PAIR_W) for blk in blocks]


def _na_attn_side(q, k, v, cache_kt, cache_vt, bias, blocks, *, layer):
    spec = _side_spec()
    n_heads = 2 * SIDE_PAIRS
    cspec = pl.BlockSpec((None, None, n_heads, HEAD_DIM, PAST_LEN),
                         lambda t, g: (g, layer, t, 0, 0))
    bspec = _resident((None, n_heads) + bias.shape[2:], lambda t, g: (layer, t, 0, 0))
    return (functools.partial(_na_attn_jobs, blocks=blocks),
            (q, k, v, cache_kt, cache_vt, bias), [spec, spec, spec, cspec, cspec, bspec],
            [jax.ShapeDtypeStruct(q.shape, BF16)], [spec])


def _mixer_out_ffn_kernel(x_ref, oa_ref, ob_ref, oc_ref, mod_ref, gain_ref, og_ref, wo_ref,
                          wg_ref, wu_ref, wd_ref, o_ref, *, side_jobs=()):
    m = mod_ref[...]
    gb = og_ref[:, POOL_WIDTH:POOL_WIDTH + NA_WIDTH]
    c0 = POOL_WIDTH
    c1 = POOL_WIDTH + NA_WIDTH
    ob = _rms(ob_ref[...].astype(F32), gb).astype(BF16)
    o = (_dot(oa_ref[...], wo_ref[0:c0, :]) + _dot(ob, wo_ref[c0:c1, :])
         + _dot(oc_ref[...], wo_ref[c1:, :]))
    x = x_ref[...] + m[5:6] * o
    o_ref[...] = _ffn_tile(x, gain_ref[2:3, :], m[6:7], m[7:8], m[8:9], wg_ref, wu_ref, wd_ref,
                           side_jobs)


def _mixer_out_ffn(x, oa, ob, oc, mods, norm_g, og, wo, wg, wu, wd, *, layer, row0, per_group,
                   sides=()):
    specs = [
        _tile_spec(D_MODEL), _tile_spec(POOL_WIDTH), _tile_spec(NA_WIDTH), _tile_spec(SG_WIDTH),
        _mod_spec(row0, per_group, group_axis=1),
        _resident((None, 3, D_MODEL), lambda t, g: (layer, 0, 0)),
        _resident((None, 1, D_MODEL), lambda t, g: (layer, 0, 0)),
        _resident((D_MODEL, D_MODEL), lambda t, g: (0, 0)),
    ] + _ffn_weight_specs(1)
    return _tile_call("mixer_out_ffn", _mixer_out_ffn_kernel,
                      (x, oa, ob, oc, mods, norm_g, og, wo, wg, wu, wd), specs, x.shape, sides)


def _block_diag(w):
    out = jnp.zeros((DEPTH, POOL_WIDTH, POOL_WIDTH), w.dtype)
    for g in range(4):
        lo = g * HEAD_DIM
        out = out.at[:, lo:lo + HEAD_DIM, lo:lo + HEAD_DIM].set(w[:, g])
    return out


def kernel(x_prompt, x_sample, cache_k, cache_v, c, c_ctx, ada_w, ada_b, norm_g, ffn_w_gate,
           ffn_w_up, ffn_w_down, w_in, pool_w, pool_scale, q_norm_g, k_norm_g, na_rpb,
           sg_vnorm_g, sg_w, sg_b, out_norm_g, w_out):
    dec_batch = x_sample.shape[0]

    cvec = jnp.concatenate([c_ctx[None], c, jnp.zeros((16 - 1 - dec_batch, D_MODEL), F32)])

    big = (ffn_w_gate, ffn_w_up, ffn_w_down, w_in, w_out)
    pool_wb = _block_diag(pool_w).astype(BF16)
    pool_s = pool_scale.reshape(DEPTH, 1, POOL_WIDTH)
    qg = jnp.tile(q_norm_g, (1, NA_HEADS)).reshape(DEPTH, 1, NA_WIDTH)
    kg = jnp.tile(k_norm_g, (1, NA_HEADS)).reshape(DEPTH, 1, NA_WIDTH)
    vng = sg_vnorm_g.reshape(DEPTH, 1, SG_WIDTH)
    sgw = sg_w.reshape(DEPTH, 4 * CHUNK, CHUNK).astype(BF16)
    sgb = jnp.repeat(jnp.swapaxes(sg_b, 1, 2), HEAD_DIM, axis=-1)
    og = out_norm_g.reshape(DEPTH, 1, D_MODEL)
    lane = jnp.arange(GROUP_COLS) // HEAD_DIM
    ones = (lane[:, None] == lane[None, :]).astype(BF16)
    rexp = jnp.repeat(jnp.swapaxes(na_rpb, 2, 3), GRID_W, axis=-1)
    rexp = jnp.pad(rexp, ((0, 0), (0, 0), (0, 1), (0, GRID_W)))
    na_blocks, na_tables, na_width = _na_plan()
    cache_kt = jnp.swapaxes(cache_k, 3, 4)
    cache_vt = jnp.swapaxes(cache_v, 3, 4)

    xp = x_prompt.reshape(N_GROUPS, GROUP, D_MODEL)
    xs = x_sample.reshape(N_GROUPS, GROUP, D_MODEL)
    ctx = dict(row0=0, per_group=0)
    lat = dict(row0=1, per_group=1)
    new_kv = ()
    all_mods, bias, (wg, wu, wd, w_in_b, wo) = _prepare(cvec, ada_w, ada_b, rexp, na_tables,
                                                        na_width, big)
    for l in range(DEPTH):
        last = l == DEPTH - 1
        mods = all_mods[l]
        xp, = _ffn(xp, mods, norm_g, wg, wu, wd, layer=l, half=0, **ctx)
        mix = (mods, norm_g, w_in_b, pool_wb, pool_s, qg, kg, vng, sgw, sgb, og, ones)
        oa_p, q, k, v, oc_p, *new_kv = _mixer_in(xp, *mix, layer=l, seq_len=SEQ, emit_kv=True,
                                                 prev_kv=new_kv, **ctx)
        xs, ob_p = _ffn(xs, mods, norm_g, wg, wu, wd, layer=l, half=0, **lat,
                        sides=[_ctx_attn_side(q, k, v)])
        oa_s, q, k, v, oc_s = _mixer_in(xs, *mix, layer=l, seq_len=GROUP, emit_kv=False, **lat)
        xp, ob_s = _mixer_out_ffn(
            xp, oa_p, ob_p, oc_p, mods, norm_g, og, wo, wg, wu, wd, layer=l, **ctx,
            sides=[_na_attn_side(q, k, v, cache_kt, cache_vt, bias, na_blocks, layer=l)])
        sides = [] if last else [_to_bf16_side(big, l + 1)]
        xs, *cast = _mixer_out_ffn(xs, oa_s, ob_s, oc_s, mods, norm_g, og, wo, wg, wu, wd,
                                   layer=l, **lat, sides=sides)
        if not last:
            wg, wu, wd, w_in_b, wo = (a.reshape(w.shape[1:]) for a, w in zip(cast, big))

    return (xp.reshape(x_prompt.shape), xs.reshape(x_sample.shape),
            jnp.swapaxes(new_kv[0], 3, 4), jnp.swapaxes(new_kv[1], 3, 4))
```

```python
import functools

import jax
import jax.numpy as jnp
from jax import lax
from jax.experimental import pallas as pl
from jax.experimental.pallas import tpu as pltpu

F32 = jnp.float32
BF16 = jnp.bfloat16

D_MODEL = 1024
D_FF = 2816
N_MOD = 9
DEPTH = 2
GROUP = 1024
N_GROUPS = 8
SEQ = 256
GRID_W = 64
GRID_ROWS = GROUP // GRID_W
POOL_WIDTH = 256
NA_WIDTH = 512
SG_WIDTH = 256
HEAD_DIM = 64
NA_HEADS = 8
HEAD_PAIRS = NA_HEADS // 2
PAIR_W = 2 * HEAD_DIM
NA_ROWS = 8
NA_COLS = 16
N_RPB_ROWS = 2 * NA_ROWS - 1
N_RPB_COLS = 2 * NA_COLS - 1
CHUNK = 128
PAST_LEN = 256
EPS = 1e-6
MASKED = -1e30
LOG2E = 1.4426950408889634
QUERY_SCALE = HEAD_DIM ** -0.5 * LOG2E

FFN_TM = 512
FFN_FC = 256
MIX_TM = 512
VMEM_LIMIT = 56 * 1024 * 1024


def _dot(a, b):
    return jnp.dot(a, b, preferred_element_type=F32)


def _dot_nt(a, b):
    return lax.dot_general(a, b, (((1,), (1,)), ((), ())), preferred_element_type=F32)


def _rms(x, gain):
    ms = jnp.mean(x * x, axis=-1, keepdims=True)
    return x * lax.rsqrt(ms + EPS) * gain


def _rms_mod(x, gain, shift, scale):
    ms = jnp.mean(x * x, axis=-1, keepdims=True)
    return x * lax.rsqrt(ms + EPS) * (gain * (1.0 + scale)) + shift


def _params(sem):
    return pltpu.CompilerParams(dimension_semantics=sem, vmem_limit_bytes=VMEM_LIMIT)


def _resident(shape, index_map):
    return pl.BlockSpec(shape, index_map, pipeline_mode=pl.Buffered(1))


ADALN_TN = 1024


def _adaln_kernel(c_ref, w_ref, b_ref, o_ref):
    s = jax.nn.silu(c_ref[...]).astype(BF16)
    o_ref[...] = _dot(s, w_ref[...].astype(BF16)) + b_ref[...]


ADALN_TILES = N_MOD * D_MODEL // ADALN_TN
PREP_STEPS = DEPTH * ADALN_TILES
PREP_PARTS = DEPTH * NA_HEADS


def _prep_kernel(c_ref, aw_ref, ab_ref, rexp_ref, *refs, tables, n_cast):
    mods_ref, bias_ref = refs[n_cast:n_cast + 2]
    _adaln_kernel(c_ref, aw_ref, ab_ref, mods_ref)
    _bias_kernel(rexp_ref, bias_ref, tables=tables)
    for src, dst in zip(refs[:n_cast], refs[n_cast + 2:]):
        dst[...] = src[...].astype(BF16)


def _prepare(cvec, ada_w, ada_b, rexp, tables, width, weights):
    rows = cvec.shape[0]
    flat = [w.reshape(-1, w.shape[-1]) for w in weights]
    w_rows = [a.shape[0] // DEPTH for a in flat]
    part = lambda s: jnp.minimum(s, PREP_PARTS - 1)
    tile = lambda s: (s // ADALN_TILES, 0, s % ADALN_TILES)
    head = lambda s: (part(s) // NA_HEADS, part(s) % NA_HEADS, 0, 0)
    slab = [pl.BlockSpec((r // PREP_PARTS, a.shape[1]), lambda s: (part(s), 0))
            for a, r in zip(flat, w_rows)]
    outs = pl.pallas_call(
        functools.partial(_prep_kernel, tables=tables, n_cast=len(flat)),
        out_shape=[jax.ShapeDtypeStruct((DEPTH, rows, N_MOD * D_MODEL), F32),
                   jax.ShapeDtypeStruct((DEPTH, NA_HEADS, width, NA_QROWS * GRID_W), F32)]
        + [jax.ShapeDtypeStruct((r, a.shape[1]), BF16) for a, r in zip(flat, w_rows)],
        grid=(PREP_STEPS,),
        in_specs=[
            pl.BlockSpec((rows, D_MODEL), lambda s: (0, 0)),
            pl.BlockSpec((None, D_MODEL, ADALN_TN), tile),
            pl.BlockSpec((None, 1, ADALN_TN), tile),
            pl.BlockSpec((None, None, N_RPB_COLS + 1, 16 * GRID_W), head),
        ] + slab,
        out_specs=[
            pl.BlockSpec((None, rows, ADALN_TN), tile),
            pl.BlockSpec((None, None, width, NA_QROWS * GRID_W), head),
        ] + slab,
        compiler_params=_params(("arbitrary",)),
        name="prepare",
    )(cvec, ada_w, ada_b.reshape(DEPTH, 1, N_MOD * D_MODEL), rexp, *flat)
    mods = outs[0].reshape(DEPTH, rows, N_MOD, D_MODEL)
    return mods, outs[1], [a.reshape(w.shape[1:]) for a, w in zip(outs[2:], weights)]


def _ffn_tile(x, gain, shift, scale, gate, wg_ref, wu_ref, wd_ref, side_jobs=()):
    h = _rms_mod(x, gain, shift, scale).astype(BF16)
    acc = jnp.zeros(x.shape, F32)
    n_chunks = D_FF // FFN_FC
    per_chunk = -(-len(side_jobs) // n_chunks)
    for c in range(n_chunks):
        finishers = [job() for job in side_jobs[c * per_chunk:(c + 1) * per_chunk]]
        lo = c * FFN_FC
        g = _dot(h, wg_ref[:, lo:lo + FFN_FC])
        u = _dot(h, wu_ref[:, lo:lo + FFN_FC])
        a = (g * jax.nn.sigmoid(g) * u).astype(BF16)
        acc = acc + _dot(a, wd_ref[lo:lo + FFN_FC, :])
        for finish in finishers:
            finish()
    return x + 0.5 * gate * acc


def _ffn_kernel(x_ref, mod_ref, gain_ref, wg_ref, wu_ref, wd_ref, o_ref, *, mod0, gain_row,
                side_jobs=()):
    m = mod_ref[...]
    o_ref[...] = _ffn_tile(x_ref[...], gain_ref[gain_row:gain_row + 1, :],
                           m[mod0:mod0 + 1], m[mod0 + 1:mod0 + 2], m[mod0 + 2:mod0 + 3],
                           wg_ref, wu_ref, wd_ref, side_jobs)


def _mod_spec(row0, per_group, group_axis=0):
    return pl.BlockSpec((None, N_MOD, D_MODEL),
                        lambda *idx: (row0 + idx[group_axis] * per_group, 0, 0))


def _ffn_weight_specs(half):
    return [
        _resident((None, D_MODEL, D_FF), lambda *_: (half, 0, 0)),
        _resident((None, D_MODEL, D_FF), lambda *_: (half, 0, 0)),
        _resident((None, D_FF, D_MODEL), lambda *_: (half, 0, 0)),
    ]


N_TILES = GROUP // FFN_TM
N_STEPS = N_TILES * N_GROUPS
SIDE_PAIRS = HEAD_PAIRS // N_TILES


def _tile_spec(width):
    return pl.BlockSpec((None, FFN_TM, width), lambda t, g: (g, t, 0))


def _step(t, g):
    return t * N_GROUPS + g


def _one_stage(fn):
    def start():
        fn()
        return lambda: None
    return start


def _joint_kernel(*refs, main, n_main, sides):
    outs = refs[n_main + sum(n_in for _, n_in, _ in sides):]
    jobs = []
    i, o = n_main, 1
    for jobs_fn, n_in, n_out in sides:
        jobs += jobs_fn(*refs[i:i + n_in], *outs[o:o + n_out])
        i, o = i + n_in, o + n_out
    main(*refs[:n_main], outs[0], side_jobs=jobs)


def _tile_call(name, main, main_args, main_specs, x_shape, sides=()):
    kern = functools.partial(
        _joint_kernel, main=main, n_main=len(main_args),
        sides=tuple((fn, len(args), len(shapes)) for fn, args, _, shapes, _ in sides))
    outs = pl.pallas_call(
        kern,
        out_shape=[jax.ShapeDtypeStruct(x_shape, F32)] + [s for side in sides for s in side[3]],
        grid=(N_TILES, N_GROUPS),
        in_specs=list(main_specs) + [s for side in sides for s in side[2]],
        out_specs=[_tile_spec(D_MODEL)] + [s for side in sides for s in side[4]],
        compiler_params=_params(("arbitrary", "arbitrary")),
        name=name,
    )(*main_args, *[a for side in sides for a in side[1]])
    return outs


def _ffn(x, mods, norm_g, wg, wu, wd, *, layer, half, row0, per_group, sides=()):
    kern = functools.partial(_ffn_kernel, mod0=6 * half, gain_row=2 * half)
    specs = [
        _tile_spec(D_MODEL),
        _mod_spec(row0, per_group, group_axis=1),
        _resident((None, 3, D_MODEL), lambda t, g: (layer, 0, 0)),
    ] + _ffn_weight_specs(half)
    return _tile_call("ffn", kern, (x, mods, norm_g, wg, wu, wd), specs, x.shape, sides)


def _to_bf16_side(arrays, layer):
    flat = [a.reshape(-1, a.shape[-1]) for a in arrays]
    rows = [a.shape[0] // DEPTH for a in flat]
    in_specs = [pl.BlockSpec((r // N_STEPS, a.shape[1]),
                             lambda t, g: (layer * N_STEPS + _step(t, g), 0))
                for a, r in zip(flat, rows)]
    out_specs = [pl.BlockSpec((r // N_STEPS, a.shape[1]), lambda t, g: (_step(t, g), 0))
                 for a, r in zip(flat, rows)]
    out_shapes = [jax.ShapeDtypeStruct((r, a.shape[1]), BF16) for a, r in zip(flat, rows)]

    def jobs(*refs):
        def cast(src, dst):
            def run():
                dst[...] = src[...].astype(BF16)
            return _one_stage(run)
        return [cast(src, dst) for src, dst in zip(refs[:len(flat)], refs[len(flat):])]

    return jobs, flat, in_specs, out_shapes, out_specs


LANES = 128
GROUP_COLS = 256


def _pool_column(a, lane0, seq_len):
    t = a.shape[0]
    pos = lax.broadcasted_iota(jnp.int32, a.shape, 0) & (seq_len - 1)
    lane = lax.broadcasted_iota(jnp.int32, (1, LANES), 1) + lane0
    half = jnp.left_shift(1, lane >> 6)
    max_half = 1 << ((lane0 + LANES - 1) // HEAD_DIM)
    right = a
    left = a
    k = 1
    while 2 * k <= max_half:
        grow = half >= 2 * k
        right = right + jnp.where(pos < jnp.where(grow, seq_len - k, 0),
                                  pltpu.roll(right, t - k, 0), 0.0)
        left = left + jnp.where(pos >= jnp.where(grow, k, seq_len),
                                pltpu.roll(left, k, 0), 0.0)
        k *= 2
    window = right + jnp.where(pos >= 1, pltpu.roll(left, 1, 0), 0.0)
    cnt = jnp.minimum(pos + half, seq_len) - jnp.maximum(pos - half, 0)
    return window / cnt.astype(F32) - a


def _pool_mixer(a, seq_len):
    cols = [_pool_column(a[:, c:c + LANES], c, seq_len) for c in range(0, a.shape[1], LANES)]
    return jnp.concatenate(cols, axis=-1)


def _group_rms(z, ones_ref):
    outs = []
    for c in range(z.shape[1] // GROUP_COLS):
        zc = z[:, c * GROUP_COLS:(c + 1) * GROUP_COLS]
        ms = _dot((zc * zc).astype(BF16), ones_ref[...]) * (1.0 / HEAD_DIM)
        outs.append(zc * lax.rsqrt(ms + EPS))
    return outs[0] if len(outs) == 1 else jnp.concatenate(outs, axis=-1)


def _mixer_in_kernel(x_ref, mod_ref, gain_ref, w_in_ref, pool_w_ref, pool_s_ref,
                     qg_ref, kg_ref, vng_ref, sgw_ref, sgb_ref, og_ref, ones_ref,
                     *refs, seq_len, n_alias):
    oa_ref, q_ref, k_ref, vt_ref, oc_ref, *kv_refs = refs[n_alias:]
    m = mod_ref[...]
    og = og_ref[...]
    gc = og[:, POOL_WIDTH + NA_WIDTH:]
    lane_grp = lax.broadcasted_iota(jnp.int32, (CHUNK, SG_WIDTH), 1) >> 6
    sgw = sgw_ref[...]
    sgb = sgb_ref[...]
    splits = (0, POOL_WIDTH, POOL_WIDTH + NA_WIDTH, POOL_WIDTH + 2 * NA_WIDTH,
              POOL_WIDTH + 3 * NA_WIDTH, POOL_WIDTH + 3 * NA_WIDTH + SG_WIDTH, w_in_ref.shape[1])

    def pool_out(rows, za):
        pooled = _pool_mixer(za, seq_len)
        a_out = _dot(pooled.astype(BF16), pool_w_ref[...]) * pool_s_ref[...]
        oa_ref[rows, :] = _rms(a_out, og[:, 0:POOL_WIDTH]).astype(BF16)

    tiles = [slice(r, r + MIX_TM) for r in range(0, GROUP, MIX_TM)]
    projected = []
    for rows in tiles:
        h = _rms_mod(x_ref[rows, :], gain_ref[1:2, :], m[3:4], m[4:5]).astype(BF16)
        projected.append([_dot(h, w_in_ref[:, a:b]) for a, b in zip(splits[:-1], splits[1:])])

    for rows, (za, zq, zk, zv, zu, zsv) in zip(tiles, projected):
        if seq_len <= MIX_TM:
            pool_out(rows, za)

        q_ref[rows, :] = (_group_rms(zq, ones_ref) * (qg_ref[...] * QUERY_SCALE)).astype(BF16)
        k = _group_rms(zk, ones_ref) * kg_ref[...]
        k_ref[rows, :] = k.astype(BF16)
        v_t = zv.T
        vt_ref[:, rows] = v_t.astype(BF16)
        for val_t, ref in zip((k.T, v_t) if kv_refs else (), kv_refs):
            for s in range(MIX_TM // SEQ):
                for hd in range(NA_HEADS):
                    ref[rows.start // SEQ + s, hd] = val_t[hd * HEAD_DIM:(hd + 1) * HEAD_DIM,
                                                           s * SEQ:(s + 1) * SEQ]

        u = jax.nn.gelu(zu)
        vg = (_group_rms(jax.nn.gelu(zsv), ones_ref) * vng_ref[...]).astype(BF16)
        for r0 in range(0, MIX_TM, CHUNK):
            full = _dot(sgw, vg[r0:r0 + CHUNK, :])
            sp = full[0:CHUNK]
            for gi in range(1, 4):
                sp = jnp.where(lane_grp == gi, full[gi * CHUNK:(gi + 1) * CHUNK], sp)
            c_out = u[r0:r0 + CHUNK, :] * (sp + sgb)
            oc_ref[rows.start + r0:rows.start + r0 + CHUNK, :] = _rms(c_out, gc).astype(BF16)

    if seq_len > MIX_TM:
        pool_out(slice(0, GROUP), jnp.concatenate([z[0] for z in projected], axis=0))


def _mixer_in(x, mods, norm_g, w_in, pool_w, pool_s, qg, kg, vng, sgw, sgb, og, ones,
              *, layer, row0, per_group, seq_len, emit_kv, prev_kv=()):
    kern = functools.partial(_mixer_in_kernel, seq_len=seq_len, n_alias=len(prev_kv))
    tok = lambda w, dt: jax.ShapeDtypeStruct((N_GROUPS, GROUP, w), dt)
    tok_spec = lambda w: pl.BlockSpec((None, GROUP, w), lambda g: (g, 0, 0))
    lay = lambda *shape: _resident((None,) + shape, lambda g: (layer,) + (0,) * len(shape))
    n_seq = GROUP // SEQ
    kv_shape = jax.ShapeDtypeStruct((N_GROUPS * n_seq, DEPTH, NA_HEADS, HEAD_DIM, SEQ), F32)
    kv_spec = pl.BlockSpec((n_seq, None, NA_HEADS, HEAD_DIM, SEQ), lambda g: (g, layer, 0, 0, 0))
    kv_shapes = (kv_shape, kv_shape) if emit_kv else ()
    kv_specs = (kv_spec, kv_spec) if emit_kv else ()
    n_in = 13
    aliases = {n_in + i: 5 + i for i in range(len(prev_kv))}
    return pl.pallas_call(
        kern,
        out_shape=(tok(POOL_WIDTH, BF16), tok(NA_WIDTH, BF16), tok(NA_WIDTH, BF16),
                   jax.ShapeDtypeStruct((N_GROUPS, NA_WIDTH, GROUP), BF16),
                   tok(SG_WIDTH, BF16)) + kv_shapes,
        grid=(N_GROUPS,),
        in_specs=[
            tok_spec(D_MODEL),
            _mod_spec(row0, per_group),
            lay(3, D_MODEL),
            _resident(w_in.shape, lambda g: (0, 0)),
            lay(POOL_WIDTH, POOL_WIDTH),
            lay(1, POOL_WIDTH),
            lay(1, NA_WIDTH),
            lay(1, NA_WIDTH),
            lay(1, SG_WIDTH),
            lay(4 * CHUNK, CHUNK),
            lay(CHUNK, SG_WIDTH),
            lay(1, D_MODEL),
            _resident((GROUP_COLS, GROUP_COLS), lambda g: (0, 0)),
        ] + [pl.BlockSpec(memory_space=pl.ANY)] * len(prev_kv),
        out_specs=(tok_spec(POOL_WIDTH), tok_spec(NA_WIDTH), tok_spec(NA_WIDTH),
                   pl.BlockSpec((None, NA_WIDTH, GROUP), lambda g: (g, 0, 0)),
                   tok_spec(SG_WIDTH)) + kv_specs,
        input_output_aliases=aliases,
        compiler_params=_params(("arbitrary",)),
        name="mixer_in",
    )(x, mods, norm_g, w_in, pool_w, pool_s, qg, kg, vng, sgw, sgb, og, ones, *prev_kv)


def _first_head(shape, axis):
    return lax.broadcasted_iota(jnp.int32, shape, axis) < HEAD_DIM


def _head_queries(q):
    first = _first_head(q.shape, 1)
    zero = jnp.zeros_like(q)
    return jnp.where(first, q, zero), jnp.where(first, zero, q)


def _head_values_t(vt):
    first = _first_head(vt.shape, 0)
    one = jnp.ones_like(vt)
    return jnp.where(first, vt, one), jnp.where(first, one, vt)


def _softmax_pv_t(scores_t, values_t):
    mx = scores_t[0].max(axis=0, keepdims=True)
    for s in scores_t[1:]:
        mx = jnp.maximum(mx, s.max(axis=0, keepdims=True))
    out = 0.0
    for s, vt in zip(scores_t, values_t):
        out = out + _dot(vt, jnp.exp2(s - mx).astype(BF16))
    return out


def _merge_heads_t(o0, o1):
    first = _first_head(o0.shape, 0)
    num = jnp.where(first, o0, o1)
    den = pltpu.roll(jnp.where(first, o1, o0), HEAD_DIM, 0)
    return (num / den).T.astype(BF16)


def _ctx_attn_jobs(q_ref, k_ref, vt_ref, o_ref):
    def job(rows, lanes):
        def start():
            k = k_ref[rows, lanes]
            q0, q1 = _head_queries(q_ref[rows, lanes])
            s0, s1 = _dot_nt(k, q0), _dot_nt(k, q1)

            def finish():
                v0, v1 = _head_values_t(vt_ref[lanes, rows])
                o_ref[rows, lanes] = _merge_heads_t(_softmax_pv_t([s0], [v0]),
                                                    _softmax_pv_t([s1], [v1]))
            return finish
        return start

    return [job(slice(s, s + SEQ), slice(p, p + PAIR_W))
            for p in range(0, q_ref.shape[1], PAIR_W) for s in range(0, GROUP, SEQ)]


def _side_spec():
    return pl.BlockSpec((None, GROUP, SIDE_PAIRS * PAIR_W), lambda t, g: (g, 0, t))


def _side_spec_t():
    return pl.BlockSpec((None, SIDE_PAIRS * PAIR_W, GROUP), lambda t, g: (g, t, 0))


def _ctx_attn_side(q, k, vt):
    spec = _side_spec()
    return (_ctx_attn_jobs, (q, k, vt), [spec, spec, _side_spec_t()],
            [jax.ShapeDtypeStruct(q.shape, BF16)], [spec])


NA_QROWS = 4


def _na_plan():
    blocks, tables, offsets, width = [], [], [], 0
    for r0 in range(0, GRID_ROWS, NA_QROWS):
        rs = range(r0, r0 + NA_QROWS)
        starts = [min(max(r - NA_ROWS // 2, 0), GRID_ROWS - NA_ROWS) for r in rs]
        nk = max(starts) + NA_ROWS - min(starts)
        nk += nk % 2
        k0 = min(min(starts), GRID_ROWS - nk)
        table = (nk, tuple((k0 - r + NA_ROWS - 1, s - k0, s - k0 + NA_ROWS)
                           for r, s in zip(rs, starts)))
        assert all(d >= 0 and d + nk <= N_RPB_ROWS + 1 for d, _, _ in table[1])
        if table not in tables:
            tables.append(table)
            offsets.append(width)
            width += nk * GRID_W
        blocks.append((r0, k0, nk, offsets[tables.index(table)]))
    return tuple(blocks), tuple(tables), width


def _bias_kernel(rexp_ref, o_ref, *, tables):
    shape = (GRID_W, 16 * GRID_W)
    cq = lax.broadcasted_iota(jnp.int32, shape, 0)
    ck = lax.broadcasted_iota(jnp.int32, shape, 1) & (GRID_W - 1)
    dc = jnp.clip(ck - cq, -(NA_COLS - 1), NA_COLS - 1) + (NA_COLS - 1)
    col_start = jnp.clip(cq - NA_COLS // 2, 0, GRID_W - NA_COLS)
    ok = (ck >= col_start) & (ck < col_start + NA_COLS)
    table = jnp.zeros(shape, F32)
    for j in range(N_RPB_COLS):
        table = jnp.where(dc == j, jnp.broadcast_to(rexp_ref[j:j + 1, :], shape), table)
    table = jnp.where(ok, table * LOG2E, MASKED)
    col0 = 0
    for nk, rows in tables:
        width = nk * GRID_W
        krow = lax.broadcasted_iota(jnp.int32, (GRID_W, width), 1) >> 6
        slabs = []
        for d, lo, hi in rows:
            slab = table[:, d * GRID_W:d * GRID_W + width]
            if lo > 0 or hi < nk:
                slab = jnp.where((krow >= lo) & (krow < hi), slab, MASKED)
            slabs.append(slab)
        o_ref[col0:col0 + width, :] = jnp.concatenate(slabs, axis=0).T
        col0 += width


def _na_attn_jobs(q_ref, k_ref, vt_ref, ck_ref, cv_ref, bias_ref, o_ref, *, blocks):
    pair_cache = {}

    def cached_pair(ref, p):
        key = (id(ref), p)
        if key not in pair_cache:
            pair_cache[key] = jnp.concatenate([ref[2 * p], ref[2 * p + 1]], axis=0).astype(BF16)
        return pair_cache[key]

    def cached_values(p):
        if ("values", p) not in pair_cache:
            pair_cache["values", p] = _head_values_t(cached_pair(cv_ref, p))
        return pair_cache["values", p]

    def job(p, r0, k0, nk, col0):
        lanes = slice(p * PAIR_W, (p + 1) * PAIR_W)
        h0 = 2 * p
        q_rows = slice(r0 * GRID_W, (r0 + NA_QROWS) * GRID_W)
        k_rows = slice(k0 * GRID_W, (k0 + nk) * GRID_W)

        def start():
            kc_t = cached_pair(ck_ref, p)
            k_loc = k_ref[k_rows, lanes]
            scores = []
            for e, qe in enumerate(_head_queries(q_ref[q_rows, lanes])):
                s_loc = _dot_nt(k_loc, qe) + bias_ref[h0 + e, col0:col0 + nk * GRID_W, :]
                scores.append([s_loc, _dot(qe, kc_t).T])

            def finish():
                vc_t = cached_values(p)
                v_loc = _head_values_t(vt_ref[lanes, k_rows])
                o_ref[q_rows, lanes] = _merge_heads_t(
                    *[_softmax_pv_t(scores[e], [v_loc[e], vc_t[e]]) for e in range(2)])
            return finish
        return start

    return [job(p, *blk) for p in range(q_ref.shape[1] // PAIR_W) for blk in blocks]


def _na_attn_side(q, k, vt, cache_kt, cache_vt, bias, blocks, *, layer):
    spec = _side_spec()
    n_heads = 2 * SIDE_PAIRS
    cspec = pl.BlockSpec((None, None, n_heads, HEAD_DIM, PAST_LEN),
                         lambda t, g: (g, layer, t, 0, 0))
    bspec = _resident((None, n_heads) + bias.shape[2:], lambda t, g: (layer, t, 0, 0))
    return (functools.partial(_na_attn_jobs, blocks=blocks),
            (q, k, vt, cache_kt, cache_vt, bias),
            [spec, spec, _side_spec_t(), cspec, cspec, bspec],
            [jax.ShapeDtypeStruct(q.shape, BF16)], [spec])


def _mixer_out_ffn_kernel(x_ref, oa_ref, ob_ref, oc_ref, mod_ref, gain_ref, og_ref, wo_ref,
                          wg_ref, wu_ref, wd_ref, o_ref, *, side_jobs=()):
    m = mod_ref[...]
    gb = og_ref[:, POOL_WIDTH:POOL_WIDTH + NA_WIDTH]
    c0 = POOL_WIDTH
    c1 = POOL_WIDTH + NA_WIDTH
    ob = _rms(ob_ref[...].astype(F32), gb).astype(BF16)
    o = (_dot(oa_ref[...], wo_ref[0:c0, :]) + _dot(ob, wo_ref[c0:c1, :])
         + _dot(oc_ref[...], wo_ref[c1:, :]))
    x = x_ref[...] + m[5:6] * o
    o_ref[...] = _ffn_tile(x, gain_ref[2:3, :], m[6:7], m[7:8], m[8:9], wg_ref, wu_ref, wd_ref,
                           side_jobs)


def _mixer_out_ffn(x, oa, ob, oc, mods, norm_g, og, wo, wg, wu, wd, *, layer, row0, per_group,
                   sides=()):
    specs = [
        _tile_spec(D_MODEL), _tile_spec(POOL_WIDTH), _tile_spec(NA_WIDTH), _tile_spec(SG_WIDTH),
        _mod_spec(row0, per_group, group_axis=1),
        _resident((None, 3, D_MODEL), lambda t, g: (layer, 0, 0)),
        _resident((None, 1, D_MODEL), lambda t, g: (layer, 0, 0)),
        _resident((D_MODEL, D_MODEL), lambda t, g: (0, 0)),
    ] + _ffn_weight_specs(1)
    return _tile_call("mixer_out_ffn", _mixer_out_ffn_kernel,
                      (x, oa, ob, oc, mods, norm_g, og, wo, wg, wu, wd), specs, x.shape, sides)


def _block_diag(w):
    out = jnp.zeros((DEPTH, POOL_WIDTH, POOL_WIDTH), w.dtype)
    for g in range(4):
        lo = g * HEAD_DIM
        out = out.at[:, lo:lo + HEAD_DIM, lo:lo + HEAD_DIM].set(w[:, g])
    return out


def kernel(x_prompt, x_sample, cache_k, cache_v, c, c_ctx, ada_w, ada_b, norm_g, ffn_w_gate,
           ffn_w_up, ffn_w_down, w_in, pool_w, pool_scale, q_norm_g, k_norm_g, na_rpb,
           sg_vnorm_g, sg_w, sg_b, out_norm_g, w_out):
    dec_batch = x_sample.shape[0]

    cvec = jnp.concatenate([c_ctx[None], c, jnp.zeros((16 - 1 - dec_batch, D_MODEL), F32)])

    big = (ffn_w_gate, ffn_w_up, ffn_w_down, w_in, w_out)
    pool_wb = _block_diag(pool_w).astype(BF16)
    pool_s = pool_scale.reshape(DEPTH, 1, POOL_WIDTH)
    qg = jnp.tile(q_norm_g, (1, NA_HEADS)).reshape(DEPTH, 1, NA_WIDTH)
    kg = jnp.tile(k_norm_g, (1, NA_HEADS)).reshape(DEPTH, 1, NA_WIDTH)
    vng = sg_vnorm_g.reshape(DEPTH, 1, SG_WIDTH)
    sgw = sg_w.reshape(DEPTH, 4 * CHUNK, CHUNK).astype(BF16)
    sgb = jnp.repeat(jnp.swapaxes(sg_b, 1, 2), HEAD_DIM, axis=-1)
    og = out_norm_g.reshape(DEPTH, 1, D_MODEL)
    lane = jnp.arange(GROUP_COLS) // HEAD_DIM
    ones = (lane[:, None] == lane[None, :]).astype(BF16)
    rexp = jnp.repeat(jnp.swapaxes(na_rpb, 2, 3), GRID_W, axis=-1)
    rexp = jnp.pad(rexp, ((0, 0), (0, 0), (0, 1), (0, GRID_W)))
    na_blocks, na_tables, na_width = _na_plan()
    cache_kt = jnp.swapaxes(cache_k, 3, 4)
    cache_vt = jnp.swapaxes(cache_v, 3, 4)

    xp = x_prompt.reshape(N_GROUPS, GROUP, D_MODEL)
    xs = x_sample.reshape(N_GROUPS, GROUP, D_MODEL)
    ctx = dict(row0=0, per_group=0)
    lat = dict(row0=1, per_group=1)
    new_kv = ()
    all_mods, bias, (wg, wu, wd, w_in_b, wo) = _prepare(cvec, ada_w, ada_b, rexp, na_tables,
                                                        na_width, big)
    for l in range(DEPTH):
        last = l == DEPTH - 1
        mods = all_mods[l]
        xp, = _ffn(xp, mods, norm_g, wg, wu, wd, layer=l, half=0, **ctx)
        mix = (mods, norm_g, w_in_b, pool_wb, pool_s, qg, kg, vng, sgw, sgb, og, ones)
        oa_p, q, k, v, oc_p, *new_kv = _mixer_in(xp, *mix, layer=l, seq_len=SEQ, emit_kv=True,
                                                 prev_kv=new_kv, **ctx)
        xs, ob_p = _ffn(xs, mods, norm_g, wg, wu, wd, layer=l, half=0, **lat,
                        sides=[_ctx_attn_side(q, k, v)])
        oa_s, q, k, v, oc_s = _mixer_in(xs, *mix, layer=l, seq_len=GROUP, emit_kv=False, **lat)
        xp, ob_s = _mixer_out_ffn(
            xp, oa_p, ob_p, oc_p, mods, norm_g, og, wo, wg, wu, wd, layer=l, **ctx,
            sides=[_na_attn_side(q, k, v, cache_kt, cache_vt, bias, na_blocks, layer=l)])
        sides = [] if last else [_to_bf16_side(big, l + 1)]
        xs, *cast = _mixer_out_ffn(xs, oa_s, ob_s, oc_s, mods, norm_g, og, wo, wg, wu, wd,
                                   layer=l, **lat, sides=sides)
        if not last:
            wg, wu, wd, w_in_b, wo = (a.reshape(w.shape[1:]) for a, w in zip(cast, big))

    return (xp.reshape(x_prompt.shape), xs.reshape(x_sample.shape),
            jnp.swapaxes(new_kv[0], 3, 4), jnp.swapaxes(new_kv[1], 3, 4))
```

```python
import functools

import jax
import jax.numpy as jnp
from jax import lax
from jax.experimental import pallas as pl
from jax.experimental.pallas import tpu as pltpu

F32 = jnp.float32
BF16 = jnp.bfloat16

D_MODEL = 1024
D_FF = 2816
N_MOD = 9
DEPTH = 2
GROUP = 1024
N_GROUPS = 8
SEQ = 256
GRID_W = 64
GRID_ROWS = GROUP // GRID_W
POOL_WIDTH = 256
NA_WIDTH = 512
SG_WIDTH = 256
HEAD_DIM = 64
NA_HEADS = 8
HEAD_PAIRS = NA_HEADS // 2
PAIR_W = 2 * HEAD_DIM
NA_ROWS = 8
NA_COLS = 16
N_RPB_ROWS = 2 * NA_ROWS - 1
N_RPB_COLS = 2 * NA_COLS - 1
CHUNK = 128
PAST_LEN = 256
EPS = 1e-6
MASKED = -1e30
LOG2E = 1.4426950408889634
QUERY_SCALE = HEAD_DIM ** -0.5 * LOG2E

FFN_TM = 512
FFN_FC = 256
MIX_TILES = (512, 512)
VMEM_LIMIT = 56 * 1024 * 1024


def _dot(a, b):
    return jnp.dot(a, b, preferred_element_type=F32)


def _dot_nt(a, b):
    return lax.dot_general(a, b, (((1,), (1,)), ((), ())), preferred_element_type=F32)


def _rms(x, gain):
    ms = jnp.mean(x * x, axis=-1, keepdims=True)
    return x * lax.rsqrt(ms + EPS) * gain


def _rms_mod(x, gain, shift, scale):
    ms = jnp.mean(x * x, axis=-1, keepdims=True)
    return x * lax.rsqrt(ms + EPS) * (gain * (1.0 + scale)) + shift


def _params(sem):
    return pltpu.CompilerParams(dimension_semantics=sem, vmem_limit_bytes=VMEM_LIMIT)


def _resident(shape, index_map):
    return pl.BlockSpec(shape, index_map, pipeline_mode=pl.Buffered(1))


ADALN_TN = 1024


def _adaln_kernel(c_ref, w_ref, b_ref, o_ref):
    s = jax.nn.silu(c_ref[...]).astype(BF16)
    o_ref[...] = _dot(s, w_ref[...].astype(BF16)) + b_ref[...]


ADALN_TILES = N_MOD * D_MODEL // ADALN_TN
PREP_STEPS = DEPTH * ADALN_TILES
PREP_PARTS = DEPTH * NA_HEADS


def _prep_kernel(c_ref, aw_ref, ab_ref, rexp_ref, *refs, tables, n_cast):
    mods_ref, bias_ref = refs[n_cast:n_cast + 2]
    _adaln_kernel(c_ref, aw_ref, ab_ref, mods_ref)
    _bias_kernel(rexp_ref, bias_ref, tables=tables)
    for src, dst in zip(refs[:n_cast], refs[n_cast + 2:]):
        dst[...] = src[...].astype(BF16)


def _prepare(cvec, ada_w, ada_b, rexp, tables, width, weights):
    rows = cvec.shape[0]
    flat = [w.reshape(-1, w.shape[-1]) for w in weights]
    w_rows = [a.shape[0] // DEPTH for a in flat]
    part = lambda s: jnp.minimum(s, PREP_PARTS - 1)
    tile = lambda s: (s // ADALN_TILES, 0, s % ADALN_TILES)
    head = lambda s: (part(s) // NA_HEADS, part(s) % NA_HEADS, 0, 0)
    slab = [pl.BlockSpec((r // PREP_PARTS, a.shape[1]), lambda s: (part(s), 0))
            for a, r in zip(flat, w_rows)]
    outs = pl.pallas_call(
        functools.partial(_prep_kernel, tables=tables, n_cast=len(flat)),
        out_shape=[jax.ShapeDtypeStruct((DEPTH, rows, N_MOD * D_MODEL), F32),
                   jax.ShapeDtypeStruct((DEPTH, NA_HEADS, width, NA_QROWS * GRID_W), F32)]
        + [jax.ShapeDtypeStruct((r, a.shape[1]), BF16) for a, r in zip(flat, w_rows)],
        grid=(PREP_STEPS,),
        in_specs=[
            pl.BlockSpec((rows, D_MODEL), lambda s: (0, 0)),
            pl.BlockSpec((None, D_MODEL, ADALN_TN), tile),
            pl.BlockSpec((None, 1, ADALN_TN), tile),
            pl.BlockSpec((None, None, N_RPB_COLS + 1, 16 * GRID_W), head),
        ] + slab,
        out_specs=[
            pl.BlockSpec((None, rows, ADALN_TN), tile),
            pl.BlockSpec((None, None, width, NA_QROWS * GRID_W), head),
        ] + slab,
        compiler_params=_params(("arbitrary",)),
        name="prepare",
    )(cvec, ada_w, ada_b.reshape(DEPTH, 1, N_MOD * D_MODEL), rexp, *flat)
    mods = outs[0].reshape(DEPTH, rows, N_MOD, D_MODEL)
    return mods, outs[1], [a.reshape(w.shape[1:]) for a, w in zip(outs[2:], weights)]


def _ffn_tile(x, gain, shift, scale, gate, wg_ref, wu_ref, wd_ref, side_jobs=()):
    h = _rms_mod(x, gain, shift, scale).astype(BF16)
    acc = jnp.zeros(x.shape, F32)
    n_chunks = D_FF // FFN_FC
    per_chunk = -(-len(side_jobs) // n_chunks)
    for c in range(n_chunks):
        finishers = [job() for job in side_jobs[c * per_chunk:(c + 1) * per_chunk]]
        lo = c * FFN_FC
        g = _dot(h, wg_ref[:, lo:lo + FFN_FC])
        u = _dot(h, wu_ref[:, lo:lo + FFN_FC])
        a = (g * jax.nn.sigmoid(g) * u).astype(BF16)
        acc = acc + _dot(a, wd_ref[lo:lo + FFN_FC, :])
        for finish in finishers:
            finish()
    return x + 0.5 * gate * acc


def _ffn_kernel(x_ref, mod_ref, gain_ref, wg_ref, wu_ref, wd_ref, o_ref, *, mod0, gain_row,
                side_jobs=()):
    m = mod_ref[...]
    o_ref[...] = _ffn_tile(x_ref[...], gain_ref[gain_row:gain_row + 1, :],
                           m[mod0:mod0 + 1], m[mod0 + 1:mod0 + 2], m[mod0 + 2:mod0 + 3],
                           wg_ref, wu_ref, wd_ref, side_jobs)


def _mod_spec(layer, row0, per_group, group_axis=0):
    return pl.BlockSpec((None, None, N_MOD, D_MODEL),
                        lambda *idx: (layer, row0 + idx[group_axis] * per_group, 0, 0))


def _ffn_weight_specs(half):
    return [
        _resident((None, D_MODEL, D_FF), lambda *_: (half, 0, 0)),
        _resident((None, D_MODEL, D_FF), lambda *_: (half, 0, 0)),
        _resident((None, D_FF, D_MODEL), lambda *_: (half, 0, 0)),
    ]


N_TILES = GROUP // FFN_TM
N_STEPS = N_TILES * N_GROUPS
SIDE_PAIRS = HEAD_PAIRS // N_TILES


def _tile_spec(width):
    return pl.BlockSpec((None, FFN_TM, width), lambda t, g: (g, t, 0))


def _step(t, g):
    return t * N_GROUPS + g


def _one_stage(fn):
    def start():
        fn()
        return lambda: None
    return start


def _joint_kernel(*refs, main, n_main, sides):
    outs = refs[n_main + sum(n_in for _, n_in, _ in sides):]
    jobs = []
    i, o = n_main, 1
    for jobs_fn, n_in, n_out in sides:
        jobs += jobs_fn(*refs[i:i + n_in], *outs[o:o + n_out])
        i, o = i + n_in, o + n_out
    main(*refs[:n_main], outs[0], side_jobs=jobs)


def _tile_call(name, main, main_args, main_specs, x_shape, sides=()):
    kern = functools.partial(
        _joint_kernel, main=main, n_main=len(main_args),
        sides=tuple((fn, len(args), len(shapes)) for fn, args, _, shapes, _ in sides))
    outs = pl.pallas_call(
        kern,
        out_shape=[jax.ShapeDtypeStruct(x_shape, F32)] + [s for side in sides for s in side[3]],
        grid=(N_TILES, N_GROUPS),
        in_specs=list(main_specs) + [s for side in sides for s in side[2]],
        out_specs=[_tile_spec(D_MODEL)] + [s for side in sides for s in side[4]],
        compiler_params=_params(("arbitrary", "arbitrary")),
        name=name,
    )(*main_args, *[a for side in sides for a in side[1]])
    return outs


def _ffn(x, mods, norm_g, wg, wu, wd, *, layer, half, row0, per_group, sides=()):
    kern = functools.partial(_ffn_kernel, mod0=6 * half, gain_row=2 * half)
    specs = [
        _tile_spec(D_MODEL),
        _mod_spec(layer, row0, per_group, group_axis=1),
        _resident((None, 3, D_MODEL), lambda t, g: (layer, 0, 0)),
    ] + _ffn_weight_specs(half)
    return _tile_call("ffn", kern, (x, mods, norm_g, wg, wu, wd), specs, x.shape, sides)


def _to_bf16_side(arrays, layer):
    flat = [a.reshape(-1, a.shape[-1]) for a in arrays]
    rows = [a.shape[0] // DEPTH for a in flat]
    in_specs = [pl.BlockSpec((r // N_STEPS, a.shape[1]),
                             lambda t, g: (layer * N_STEPS + _step(t, g), 0))
                for a, r in zip(flat, rows)]
    out_specs = [pl.BlockSpec((r // N_STEPS, a.shape[1]), lambda t, g: (_step(t, g), 0))
                 for a, r in zip(flat, rows)]
    out_shapes = [jax.ShapeDtypeStruct((r, a.shape[1]), BF16) for a, r in zip(flat, rows)]

    def jobs(*refs):
        def cast(src, dst):
            def run():
                dst[...] = src[...].astype(BF16)
            return _one_stage(run)
        return [cast(src, dst) for src, dst in zip(refs[:len(flat)], refs[len(flat):])]

    return jobs, flat, in_specs, out_shapes, out_specs


LANES = 128
GROUP_COLS = 256


def _pool_column(a, lane0, seq_len):
    t = a.shape[0]
    pos = lax.broadcasted_iota(jnp.int32, a.shape, 0) & (seq_len - 1)
    lane = lax.broadcasted_iota(jnp.int32, (1, LANES), 1) + lane0
    half = jnp.left_shift(1, lane >> 6)
    max_half = 1 << ((lane0 + LANES - 1) // HEAD_DIM)
    right = a
    left = a
    k = 1
    while 2 * k <= max_half:
        grow = half >= 2 * k
        right = right + jnp.where(pos < jnp.where(grow, seq_len - k, 0),
                                  pltpu.roll(right, t - k, 0), 0.0)
        left = left + jnp.where(pos >= jnp.where(grow, k, seq_len),
                                pltpu.roll(left, k, 0), 0.0)
        k *= 2
    window = right + jnp.where(pos >= 1, pltpu.roll(left, 1, 0), 0.0)
    cnt = jnp.minimum(pos + half, seq_len) - jnp.maximum(pos - half, 0)
    return window / cnt.astype(F32) - a


def _pool_mixer(a, seq_len):
    cols = [_pool_column(a[:, c:c + LANES], c, seq_len) for c in range(0, a.shape[1], LANES)]
    return jnp.concatenate(cols, axis=-1)


def _group_rms(z, ones_ref):
    outs = []
    for c in range(z.shape[1] // GROUP_COLS):
        zc = z[:, c * GROUP_COLS:(c + 1) * GROUP_COLS]
        ms = _dot((zc * zc).astype(BF16), ones_ref[...]) * (1.0 / HEAD_DIM)
        outs.append(zc * lax.rsqrt(ms + EPS))
    return outs[0] if len(outs) == 1 else jnp.concatenate(outs, axis=-1)


def _mixer_in_kernel(x_ref, mod_ref, gain_ref, w_in_ref, pool_w_ref, pool_s_ref,
                     qg_ref, kg_ref, vng_ref, sgw_ref, sgb_ref, og_ref, ones_ref,
                     *refs, seq_len, n_alias):
    oa_ref, q_ref, k_ref, vt_ref, oc_ref, *kv_refs = refs[n_alias:]
    m = mod_ref[...]
    og = og_ref[...]
    gc = og[:, POOL_WIDTH + NA_WIDTH:]
    lane_grp = lax.broadcasted_iota(jnp.int32, (CHUNK, SG_WIDTH), 1) >> 6
    sgw = sgw_ref[...]
    sgb = sgb_ref[...]
    splits = (0, POOL_WIDTH, POOL_WIDTH + NA_WIDTH, POOL_WIDTH + 2 * NA_WIDTH,
              POOL_WIDTH + 3 * NA_WIDTH, POOL_WIDTH + 3 * NA_WIDTH + SG_WIDTH, w_in_ref.shape[1])

    def pool_out(rows, za):
        pooled = _pool_mixer(za, seq_len)
        a_out = _dot(pooled.astype(BF16), pool_w_ref[...]) * pool_s_ref[...]
        oa_ref[rows, :] = _rms(a_out, og[:, 0:POOL_WIDTH]).astype(BF16)

    starts = [sum(MIX_TILES[:i]) for i in range(len(MIX_TILES))]
    tiles = [slice(r, r + n) for r, n in zip(starts, MIX_TILES)]
    order = (5, 4, 0, 1, 2, 3)
    projected = []
    for rows in tiles:
        h = _rms_mod(x_ref[rows, :], gain_ref[1:2, :], m[3:4], m[4:5]).astype(BF16)
        z = {i: _dot(h, w_in_ref[:, splits[i]:splits[i + 1]]) for i in order}
        projected.append([z[i] for i in range(6)])

    for rows, (za, zq, zk, zv, zu, zsv) in zip(tiles, projected):
        u = jax.nn.gelu(zu)
        vg = (_group_rms(jax.nn.gelu(zsv), ones_ref) * vng_ref[...]).astype(BF16)
        for r0 in range(0, rows.stop - rows.start, CHUNK):
            full = _dot(sgw, vg[r0:r0 + CHUNK, :])
            sp = full[0:CHUNK]
            for gi in range(1, 4):
                sp = jnp.where(lane_grp == gi, full[gi * CHUNK:(gi + 1) * CHUNK], sp)
            c_out = u[r0:r0 + CHUNK, :] * (sp + sgb)
            oc_ref[rows.start + r0:rows.start + r0 + CHUNK, :] = _rms(c_out, gc).astype(BF16)

        if seq_len <= min(MIX_TILES):
            pool_out(rows, za)

        q_ref[rows, :] = (_group_rms(zq, ones_ref) * (qg_ref[...] * QUERY_SCALE)).astype(BF16)
        k = _group_rms(zk, ones_ref) * kg_ref[...]
        k_ref[rows, :] = k.astype(BF16)
        v_t = zv.T
        vt_ref[:, rows] = v_t.astype(BF16)
        for val_t, ref in zip((k.T, v_t) if kv_refs else (), kv_refs):
            for s in range((rows.stop - rows.start) // SEQ):
                for hd in range(NA_HEADS):
                    ref[rows.start // SEQ + s, hd] = val_t[hd * HEAD_DIM:(hd + 1) * HEAD_DIM,
                                                           s * SEQ:(s + 1) * SEQ]

    if seq_len > min(MIX_TILES):
        pool_out(slice(0, GROUP), jnp.concatenate([z[0] for z in projected], axis=0))


def _mixer_in(x, mods, norm_g, w_in, pool_w, pool_s, qg, kg, vng, sgw, sgb, og, ones,
              *, layer, row0, per_group, seq_len, emit_kv, prev_kv=()):
    kern = functools.partial(_mixer_in_kernel, seq_len=seq_len, n_alias=len(prev_kv))
    tok = lambda w, dt: jax.ShapeDtypeStruct((N_GROUPS, GROUP, w), dt)
    tok_spec = lambda w: pl.BlockSpec((None, GROUP, w), lambda g: (g, 0, 0))
    lay = lambda *shape: _resident((None,) + shape, lambda g: (layer,) + (0,) * len(shape))
    n_seq = GROUP // SEQ
    kv_shape = jax.ShapeDtypeStruct((N_GROUPS * n_seq, DEPTH, NA_HEADS, HEAD_DIM, SEQ), F32)
    kv_spec = pl.BlockSpec((n_seq, None, NA_HEADS, HEAD_DIM, SEQ), lambda g: (g, layer, 0, 0, 0))
    kv_shapes = (kv_shape, kv_shape) if emit_kv else ()
    kv_specs = (kv_spec, kv_spec) if emit_kv else ()
    n_in = 13
    aliases = {n_in + i: 5 + i for i in range(len(prev_kv))}
    return pl.pallas_call(
        kern,
        out_shape=(tok(POOL_WIDTH, BF16), tok(NA_WIDTH, BF16), tok(NA_WIDTH, BF16),
                   jax.ShapeDtypeStruct((N_GROUPS, NA_WIDTH, GROUP), BF16),
                   tok(SG_WIDTH, BF16)) + kv_shapes,
        grid=(N_GROUPS,),
        in_specs=[
            tok_spec(D_MODEL),
            _mod_spec(layer, row0, per_group),
            lay(3, D_MODEL),
            _resident(w_in.shape, lambda g: (0, 0)),
            lay(POOL_WIDTH, POOL_WIDTH),
            lay(1, POOL_WIDTH),
            lay(1, NA_WIDTH),
            lay(1, NA_WIDTH),
            lay(1, SG_WIDTH),
            lay(4 * CHUNK, CHUNK),
            lay(CHUNK, SG_WIDTH),
            lay(1, D_MODEL),
            _resident((GROUP_COLS, GROUP_COLS), lambda g: (0, 0)),
        ] + [pl.BlockSpec(memory_space=pl.ANY)] * len(prev_kv),
        out_specs=(tok_spec(POOL_WIDTH), tok_spec(NA_WIDTH), tok_spec(NA_WIDTH),
                   pl.BlockSpec((None, NA_WIDTH, GROUP), lambda g: (g, 0, 0)),
                   tok_spec(SG_WIDTH)) + kv_specs,
        input_output_aliases=aliases,
        compiler_params=_params(("arbitrary",)),
        name="mixer_in",
    )(x, mods, norm_g, w_in, pool_w, pool_s, qg, kg, vng, sgw, sgb, og, ones, *prev_kv)


def _first_head(shape, axis):
    return lax.broadcasted_iota(jnp.int32, shape, axis) < HEAD_DIM


def _head_queries(q):
    first = _first_head(q.shape, 1)
    zero = jnp.zeros_like(q)
    return jnp.where(first, q, zero), jnp.where(first, zero, q)


def _head_values_t(vt):
    first = _first_head(vt.shape, 0)
    one = jnp.ones_like(vt)
    return jnp.where(first, vt, one), jnp.where(first, one, vt)


def _softmax_pv_t(scores_t, values_t):
    mx = scores_t[0].max(axis=0, keepdims=True)
    for s in scores_t[1:]:
        mx = jnp.maximum(mx, s.max(axis=0, keepdims=True))
    out = 0.0
    for s, vt in zip(scores_t, values_t):
        out = out + _dot(vt, jnp.exp2(s - mx).astype(BF16))
    return out


def _merge_heads_t(o0, o1):
    first = _first_head(o0.shape, 0)
    num = jnp.where(first, o0, o1)
    den = pltpu.roll(jnp.where(first, o1, o0), HEAD_DIM, 0)
    return (num / den).T.astype(BF16)


def _ctx_attn_jobs(q_ref, k_ref, vt_ref, o_ref):
    def job(rows, lanes):
        def start():
            k = k_ref[rows, lanes]
            q0, q1 = _head_queries(q_ref[rows, lanes])
            s0, s1 = _dot_nt(k, q0), _dot_nt(k, q1)

            def finish():
                v0, v1 = _head_values_t(vt_ref[lanes, rows])
                o_ref[rows, lanes] = _merge_heads_t(_softmax_pv_t([s0], [v0]),
                                                    _softmax_pv_t([s1], [v1]))
            return finish
        return start

    return [job(slice(s, s + SEQ), slice(p, p + PAIR_W))
            for p in range(0, q_ref.shape[1], PAIR_W) for s in range(0, GROUP, SEQ)]


def _side_spec():
    return pl.BlockSpec((None, GROUP, SIDE_PAIRS * PAIR_W), lambda t, g: (g, 0, t))


def _side_spec_t():
    return pl.BlockSpec((None, SIDE_PAIRS * PAIR_W, GROUP), lambda t, g: (g, t, 0))


def _ctx_attn_side(q, k, vt):
    spec = _side_spec()
    return (_ctx_attn_jobs, (q, k, vt), [spec, spec, _side_spec_t()],
            [jax.ShapeDtypeStruct(q.shape, BF16)], [spec])


NA_QROWS = 4


def _na_plan():
    blocks, tables, offsets, width = [], [], [], 0
    for r0 in range(0, GRID_ROWS, NA_QROWS):
        rs = range(r0, r0 + NA_QROWS)
        starts = [min(max(r - NA_ROWS // 2, 0), GRID_ROWS - NA_ROWS) for r in rs]
        nk = max(starts) + NA_ROWS - min(starts)
        nk += nk % 2
        k0 = min(min(starts), GRID_ROWS - nk)
        table = (nk, tuple((k0 - r + NA_ROWS - 1, s - k0, s - k0 + NA_ROWS)
                           for r, s in zip(rs, starts)))
        assert all(d >= 0 and d + nk <= N_RPB_ROWS + 1 for d, _, _ in table[1])
        if table not in tables:
            tables.append(table)
            offsets.append(width)
            width += nk * GRID_W
        blocks.append((r0, k0, nk, offsets[tables.index(table)]))
    return tuple(blocks), tuple(tables), width


def _bias_kernel(rexp_ref, o_ref, *, tables):
    shape = (GRID_W, 16 * GRID_W)
    cq = lax.broadcasted_iota(jnp.int32, shape, 0)
    ck = lax.broadcasted_iota(jnp.int32, shape, 1) & (GRID_W - 1)
    dc = jnp.clip(ck - cq, -(NA_COLS - 1), NA_COLS - 1) + (NA_COLS - 1)
    col_start = jnp.clip(cq - NA_COLS // 2, 0, GRID_W - NA_COLS)
    ok = (ck >= col_start) & (ck < col_start + NA_COLS)
    table = jnp.zeros(shape, F32)
    for j in range(N_RPB_COLS):
        table = jnp.where(dc == j, jnp.broadcast_to(rexp_ref[j:j + 1, :], shape), table)
    table = jnp.where(ok, table * LOG2E, MASKED)
    col0 = 0
    for nk, rows in tables:
        width = nk * GRID_W
        krow = lax.broadcasted_iota(jnp.int32, (GRID_W, width), 1) >> 6
        slabs = []
        for d, lo, hi in rows:
            slab = table[:, d * GRID_W:d * GRID_W + width]
            if lo > 0 or hi < nk:
                slab = jnp.where((krow >= lo) & (krow < hi), slab, MASKED)
            slabs.append(slab)
        o_ref[col0:col0 + width, :] = jnp.concatenate(slabs, axis=0).T
        col0 += width


def _na_attn_jobs(q_ref, k_ref, vt_ref, ck_ref, cv_ref, bias_ref, o_ref, *, blocks):
    pair_cache = {}

    def cached_pair(ref, p):
        key = (id(ref), p)
        if key not in pair_cache:
            pair_cache[key] = jnp.concatenate([ref[2 * p], ref[2 * p + 1]], axis=0).astype(BF16)
        return pair_cache[key]

    def cached_values(p):
        if ("values", p) not in pair_cache:
            pair_cache["values", p] = _head_values_t(cached_pair(cv_ref, p))
        return pair_cache["values", p]

    def job(p, r0, k0, nk, col0):
        lanes = slice(p * PAIR_W, (p + 1) * PAIR_W)
        h0 = 2 * p
        q_rows = slice(r0 * GRID_W, (r0 + NA_QROWS) * GRID_W)
        k_rows = slice(k0 * GRID_W, (k0 + nk) * GRID_W)

        def start():
            kc_t = cached_pair(ck_ref, p)
            k_loc = k_ref[k_rows, lanes]
            scores = []
            for e, qe in enumerate(_head_queries(q_ref[q_rows, lanes])):
                s_loc = _dot_nt(k_loc, qe) + bias_ref[h0 + e, col0:col0 + nk * GRID_W, :]
                scores.append([s_loc, _dot(qe, kc_t).T])

            def finish():
                vc_t = cached_values(p)
                v_loc = _head_values_t(vt_ref[lanes, k_rows])
                o_ref[q_rows, lanes] = _merge_heads_t(
                    *[_softmax_pv_t(scores[e], [v_loc[e], vc_t[e]]) for e in range(2)])
            return finish
        return start

    return [job(p, *blk) for p in range(q_ref.shape[1] // PAIR_W) for blk in blocks]


def _na_attn_side(q, k, vt, cache_kt, cache_vt, bias, blocks, *, layer):
    spec = _side_spec()
    n_heads = 2 * SIDE_PAIRS
    cspec = pl.BlockSpec((None, None, n_heads, HEAD_DIM, PAST_LEN),
                         lambda t, g: (g, layer, t, 0, 0))
    bspec = _resident((None, n_heads) + bias.shape[2:], lambda t, g: (layer, t, 0, 0))
    return (functools.partial(_na_attn_jobs, blocks=blocks),
            (q, k, vt, cache_kt, cache_vt, bias),
            [spec, spec, _side_spec_t(), cspec, cspec, bspec],
            [jax.ShapeDtypeStruct(q.shape, BF16)], [spec])


def _mixer_out_ffn_kernel(x_ref, oa_ref, ob_ref, oc_ref, mod_ref, gain_ref, og_ref, wo_ref,
                          wg_ref, wu_ref, wd_ref, o_ref, *, side_jobs=()):
    m = mod_ref[...]
    gb = og_ref[:, POOL_WIDTH:POOL_WIDTH + NA_WIDTH]
    c0 = POOL_WIDTH
    c1 = POOL_WIDTH + NA_WIDTH
    ob = _rms(ob_ref[...].astype(F32), gb).astype(BF16)
    o = (_dot(oa_ref[...], wo_ref[0:c0, :]) + _dot(ob, wo_ref[c0:c1, :])
         + _dot(oc_ref[...], wo_ref[c1:, :]))
    x = x_ref[...] + m[5:6] * o
    o_ref[...] = _ffn_tile(x, gain_ref[2:3, :], m[6:7], m[7:8], m[8:9], wg_ref, wu_ref, wd_ref,
                           side_jobs)


def _mixer_out_ffn(x, oa, ob, oc, mods, norm_g, og, wo, wg, wu, wd, *, layer, row0, per_group,
                   sides=()):
    specs = [
        _tile_spec(D_MODEL), _tile_spec(POOL_WIDTH), _tile_spec(NA_WIDTH), _tile_spec(SG_WIDTH),
        _mod_spec(layer, row0, per_group, group_axis=1),
        _resident((None, 3, D_MODEL), lambda t, g: (layer, 0, 0)),
        _resident((None, 1, D_MODEL), lambda t, g: (layer, 0, 0)),
        _resident((D_MODEL, D_MODEL), lambda t, g: (0, 0)),
    ] + _ffn_weight_specs(1)
    return _tile_call("mixer_out_ffn", _mixer_out_ffn_kernel,
                      (x, oa, ob, oc, mods, norm_g, og, wo, wg, wu, wd), specs, x.shape, sides)


def _block_diag(w):
    depth, groups, d, _ = w.shape
    same = jnp.arange(groups)[:, None] == jnp.arange(groups)[None, :]
    blocks = jnp.where(same[None, :, None, :, None], w[:, :, :, None, :], 0.0)
    return blocks.reshape(depth, groups * d, groups * d)


def kernel(x_prompt, x_sample, cache_k, cache_v, c, c_ctx, ada_w, ada_b, norm_g, ffn_w_gate,
           ffn_w_up, ffn_w_down, w_in, pool_w, pool_scale, q_norm_g, k_norm_g, na_rpb,
           sg_vnorm_g, sg_w, sg_b, out_norm_g, w_out):
    cvec = jnp.concatenate([c_ctx[None], c])

    big = (ffn_w_gate, ffn_w_up, ffn_w_down, w_in, w_out)
    pool_wb = _block_diag(pool_w).astype(BF16)
    pool_s = pool_scale.reshape(DEPTH, 1, POOL_WIDTH)
    qg = jnp.tile(q_norm_g, (1, NA_HEADS)).reshape(DEPTH, 1, NA_WIDTH)
    kg = jnp.tile(k_norm_g, (1, NA_HEADS)).reshape(DEPTH, 1, NA_WIDTH)
    vng = sg_vnorm_g.reshape(DEPTH, 1, SG_WIDTH)
    sgw = sg_w.reshape(DEPTH, 4 * CHUNK, CHUNK).astype(BF16)
    sgb = jnp.repeat(jnp.swapaxes(sg_b, 1, 2), HEAD_DIM, axis=-1)
    og = out_norm_g.reshape(DEPTH, 1, D_MODEL)
    lane = jnp.arange(GROUP_COLS) // HEAD_DIM
    ones = (lane[:, None] == lane[None, :]).astype(BF16)
    rexp = jnp.repeat(jnp.swapaxes(na_rpb, 2, 3), GRID_W, axis=-1)
    rexp = jnp.pad(rexp, ((0, 0), (0, 0), (0, 1), (0, GRID_W)))
    na_blocks, na_tables, na_width = _na_plan()
    cache_kt = jnp.swapaxes(cache_k, 3, 4)
    cache_vt = jnp.swapaxes(cache_v, 3, 4)

    xp = x_prompt.reshape(N_GROUPS, GROUP, D_MODEL)
    xs = x_sample.reshape(N_GROUPS, GROUP, D_MODEL)
    ctx = dict(row0=0, per_group=0)
    lat = dict(row0=1, per_group=1)
    new_kv = ()
    mods, bias, (wg, wu, wd, w_in_b, wo) = _prepare(cvec, ada_w, ada_b, rexp, na_tables,
                                                    na_width, big)
    for l in range(DEPTH):
        last = l == DEPTH - 1
        xp, = _ffn(xp, mods, norm_g, wg, wu, wd, layer=l, half=0, **ctx)
        mix = (mods, norm_g, w_in_b, pool_wb, pool_s, qg, kg, vng, sgw, sgb, og, ones)
        oa_p, q, k, v, oc_p, *new_kv = _mixer_in(xp, *mix, layer=l, seq_len=SEQ, emit_kv=True,
                                                 prev_kv=new_kv, **ctx)
        xs, ob_p = _ffn(xs, mods, norm_g, wg, wu, wd, layer=l, half=0, **lat,
                        sides=[_ctx_attn_side(q, k, v)])
        oa_s, q, k, v, oc_s = _mixer_in(xs, *mix, layer=l, seq_len=GROUP, emit_kv=False, **lat)
        xp, ob_s = _mixer_out_ffn(
            xp, oa_p, ob_p, oc_p, mods, norm_g, og, wo, wg, wu, wd, layer=l, **ctx,
            sides=[_na_attn_side(q, k, v, cache_kt, cache_vt, bias, na_blocks, layer=l)])
        sides = [] if last else [_to_bf16_side(big, l + 1)]
        xs, *cast = _mixer_out_ffn(xs, oa_s, ob_s, oc_s, mods, norm_g, og, wo, wg, wu, wd,
                                   layer=l, **lat, sides=sides)
        if not last:
            wg, wu, wd, w_in_b, wo = (a.reshape(w.shape[1:]) for a, w in zip(cast, big))

    return (xp.reshape(x_prompt.shape), xs.reshape(x_sample.shape),
            jnp.swapaxes(new_kv[0], 3, 4), jnp.swapaxes(new_kv[1], 3, 4))
```

```python
import functools

import jax
import jax.numpy as jnp
from jax import lax
from jax.experimental import pallas as pl
from jax.experimental.pallas import tpu as pltpu

F32 = jnp.float32
BF16 = jnp.bfloat16

D_MODEL = 1024
D_FF = 2816
N_MOD = 9
DEPTH = 2
GROUP = 1024
N_GROUPS = 8
SEQ = 256
GRID_W = 64
GRID_ROWS = GROUP // GRID_W
POOL_WIDTH = 256
NA_WIDTH = 512
SG_WIDTH = 256
HEAD_DIM = 64
NA_HEADS = 8
HEAD_PAIRS = NA_HEADS // 2
PAIR_W = 2 * HEAD_DIM
NA_ROWS = 8
NA_COLS = 16
N_RPB_ROWS = 2 * NA_ROWS - 1
N_RPB_COLS = 2 * NA_COLS - 1
CHUNK = 128
PAST_LEN = 256
EPS = 1e-6
MASKED = -1e30
LOG2E = 1.4426950408889634
QUERY_SCALE = HEAD_DIM ** -0.5 * LOG2E

FFN_TM = 512
FFN_FC = 256
MIX_TILES = (512, 512)
VMEM_LIMIT = 56 * 1024 * 1024


def _dot(a, b):
    return jnp.dot(a, b, preferred_element_type=F32)


def _dot_nt(a, b):
    return lax.dot_general(a, b, (((1,), (1,)), ((), ())), preferred_element_type=F32)


def _rms(x, gain):
    ms = jnp.mean(x * x, axis=-1, keepdims=True)
    return x * lax.rsqrt(ms + EPS) * gain


def _rms_mod(x, gain, shift, scale):
    ms = jnp.mean(x * x, axis=-1, keepdims=True)
    return x * lax.rsqrt(ms + EPS) * (gain * (1.0 + scale)) + shift


def _params(sem):
    return pltpu.CompilerParams(dimension_semantics=sem, vmem_limit_bytes=VMEM_LIMIT)


def _resident(shape, index_map):
    return pl.BlockSpec(shape, index_map, pipeline_mode=pl.Buffered(1))


ADALN_TN = 1024


def _adaln_kernel(c_ref, w_ref, b_ref, o_ref):
    s = jax.nn.silu(c_ref[...]).astype(BF16)
    o_ref[...] = _dot(s, w_ref[...].astype(BF16)) + b_ref[...]


ADALN_TILES = N_MOD * D_MODEL // ADALN_TN
PREP_STEPS = DEPTH * ADALN_TILES
PREP_PARTS = DEPTH * NA_HEADS


def _prep_kernel(c_ref, aw_ref, ab_ref, rexp_ref, *refs, tables, n_cast):
    mods_ref, bias_ref = refs[n_cast:n_cast + 2]
    _adaln_kernel(c_ref, aw_ref, ab_ref, mods_ref)
    _bias_kernel(rexp_ref, bias_ref, tables=tables)
    for src, dst in zip(refs[:n_cast], refs[n_cast + 2:]):
        dst[...] = src[...].astype(BF16)


def _cast_specs(parts, n_steps, step_of):
    arrays, in_specs, out_shapes, out_specs = [], [], [], []
    for mat, first, rows in parts:
        height = rows // n_steps
        base = first // height
        arrays.append(mat)
        in_specs.append(pl.BlockSpec((height, mat.shape[1]),
                                     lambda *idx, base=base: (base + step_of(*idx), 0)))
        out_shapes.append(jax.ShapeDtypeStruct((rows, mat.shape[1]), BF16))
        out_specs.append(pl.BlockSpec((height, mat.shape[1]), lambda *idx: (step_of(*idx), 0)))
    return arrays, in_specs, out_shapes, out_specs


def _prepare(cvec, ada_w, ada_b, rexp, tables, width, cast_parts):
    rows = cvec.shape[0]
    part = lambda s: jnp.minimum(s, PREP_PARTS - 1)
    tile = lambda s: (s // ADALN_TILES, 0, s % ADALN_TILES)
    head = lambda s: (part(s) // NA_HEADS, part(s) % NA_HEADS, 0, 0)
    mats, cast_in, cast_shapes, cast_out = _cast_specs(cast_parts, PREP_PARTS, part)
    outs = pl.pallas_call(
        functools.partial(_prep_kernel, tables=tables, n_cast=len(mats)),
        out_shape=[jax.ShapeDtypeStruct((DEPTH, rows, N_MOD * D_MODEL), F32),
                   jax.ShapeDtypeStruct((DEPTH, NA_HEADS, width, NA_QROWS * GRID_W), F32)]
        + cast_shapes,
        grid=(PREP_STEPS,),
        in_specs=[
            pl.BlockSpec((rows, D_MODEL), lambda s: (0, 0)),
            pl.BlockSpec((None, D_MODEL, ADALN_TN), tile),
            pl.BlockSpec((None, 1, ADALN_TN), tile),
            pl.BlockSpec((None, None, N_RPB_COLS + 1, 16 * GRID_W), head),
        ] + cast_in,
        out_specs=[
            pl.BlockSpec((None, rows, ADALN_TN), tile),
            pl.BlockSpec((None, None, width, NA_QROWS * GRID_W), head),
        ] + cast_out,
        compiler_params=_params(("arbitrary",)),
        name="prepare",
    )(cvec, ada_w, ada_b.reshape(DEPTH, 1, N_MOD * D_MODEL), rexp, *mats)
    return outs[0].reshape(DEPTH, rows, N_MOD, D_MODEL), outs[1], outs[2:]


def _ffn_tile(x, gain, shift, scale, gate, wg_ref, wu_ref, wd_ref, side_jobs=()):
    h = _rms_mod(x, gain, shift, scale).astype(BF16)
    acc = jnp.zeros(x.shape, F32)
    n_chunks = D_FF // FFN_FC
    per_chunk = -(-len(side_jobs) // n_chunks)
    for c in range(n_chunks):
        finishers = [job() for job in side_jobs[c * per_chunk:(c + 1) * per_chunk]]
        lo = c * FFN_FC
        g = _dot(h, wg_ref[:, lo:lo + FFN_FC])
        u = _dot(h, wu_ref[:, lo:lo + FFN_FC])
        a = (g * jax.nn.sigmoid(g) * u).astype(BF16)
        acc = acc + _dot(a, wd_ref[lo:lo + FFN_FC, :])
        for finish in finishers:
            finish()
    return x + 0.5 * gate * acc


def _ffn_kernel(x_ref, mod_ref, gain_ref, wg_ref, wu_ref, wd_ref, o_ref, *, mod0, gain_row,
                side_jobs=()):
    m = mod_ref[...]
    o_ref[...] = _ffn_tile(x_ref[...], gain_ref[gain_row:gain_row + 1, :],
                           m[mod0:mod0 + 1], m[mod0 + 1:mod0 + 2], m[mod0 + 2:mod0 + 3],
                           wg_ref, wu_ref, wd_ref, side_jobs)


def _mod_spec(layer, row0, per_group, group_axis=0):
    return pl.BlockSpec((None, None, N_MOD, D_MODEL),
                        lambda *idx: (layer, row0 + idx[group_axis] * per_group, 0, 0))


def _whole(array):
    return _resident(array.shape, lambda *_: (0, 0))


N_TILES = GROUP // FFN_TM
N_STEPS = N_TILES * N_GROUPS
SIDE_PAIRS = HEAD_PAIRS // N_TILES


def _tile_spec(width):
    return pl.BlockSpec((None, FFN_TM, width), lambda t, g: (g, t, 0))


def _step(t, g):
    return t * N_GROUPS + g


def _one_stage(fn):
    def start():
        fn()
        return lambda: None
    return start


def _joint_kernel(*refs, main, n_main, sides):
    outs = refs[n_main + sum(n_in for _, n_in, _ in sides):]
    jobs = []
    i, o = n_main, 1
    for jobs_fn, n_in, n_out in sides:
        jobs += jobs_fn(*refs[i:i + n_in], *outs[o:o + n_out])
        i, o = i + n_in, o + n_out
    main(*refs[:n_main], outs[0], side_jobs=jobs)


def _tile_call(name, main, main_args, main_specs, x_shape, sides=()):
    kern = functools.partial(
        _joint_kernel, main=main, n_main=len(main_args),
        sides=tuple((fn, len(args), len(shapes)) for fn, args, _, shapes, _ in sides))
    outs = pl.pallas_call(
        kern,
        out_shape=[jax.ShapeDtypeStruct(x_shape, F32)] + [s for side in sides for s in side[3]],
        grid=(N_TILES, N_GROUPS),
        in_specs=list(main_specs) + [s for side in sides for s in side[2]],
        out_specs=[_tile_spec(D_MODEL)] + [s for side in sides for s in side[4]],
        compiler_params=_params(("arbitrary", "arbitrary")),
        name=name,
    )(*main_args, *[a for side in sides for a in side[1]])
    return outs


def _ffn(x, mods, norm_g, wg, wu, wd, *, layer, half, row0, per_group, sides=()):
    kern = functools.partial(_ffn_kernel, mod0=6 * half, gain_row=2 * half)
    specs = [
        _tile_spec(D_MODEL),
        _mod_spec(layer, row0, per_group, group_axis=1),
        _resident((None, 3, D_MODEL), lambda t, g: (layer, 0, 0)),
        _whole(wg), _whole(wu), _whole(wd),
    ]
    return _tile_call("ffn", kern, (x, mods, norm_g, wg, wu, wd), specs, x.shape, sides)


def _to_bf16_side(cast_parts):
    mats, in_specs, out_shapes, out_specs = _cast_specs(cast_parts, N_STEPS, _step)

    def jobs(*refs):
        def cast(src, dst):
            def run():
                dst[...] = src[...].astype(BF16)
            return _one_stage(run)
        return [cast(src, dst) for src, dst in zip(refs[:len(mats)], refs[len(mats):])]

    return jobs, mats, in_specs, out_shapes, out_specs


LANES = 128
GROUP_COLS = 256


def _pool_column(a, lane0, seq_len):
    t = a.shape[0]
    pos = lax.broadcasted_iota(jnp.int32, a.shape, 0) & (seq_len - 1)
    lane = lax.broadcasted_iota(jnp.int32, (1, LANES), 1) + lane0
    half = jnp.left_shift(1, lane >> 6)
    max_half = 1 << ((lane0 + LANES - 1) // HEAD_DIM)
    right = a
    left = a
    k = 1
    while 2 * k <= max_half:
        grow = half >= 2 * k
        right = right + jnp.where(pos < jnp.where(grow, seq_len - k, 0),
                                  pltpu.roll(right, t - k, 0), 0.0)
        left = left + jnp.where(pos >= jnp.where(grow, k, seq_len),
                                pltpu.roll(left, k, 0), 0.0)
        k *= 2
    window = right + jnp.where(pos >= 1, pltpu.roll(left, 1, 0), 0.0)
    cnt = jnp.minimum(pos + half, seq_len) - jnp.maximum(pos - half, 0)
    return window / cnt.astype(F32) - a


def _pool_mixer(a, seq_len):
    cols = [_pool_column(a[:, c:c + LANES], c, seq_len) for c in range(0, a.shape[1], LANES)]
    return jnp.concatenate(cols, axis=-1)


def _group_rms(z, ones_ref):
    outs = []
    for c in range(z.shape[1] // GROUP_COLS):
        zc = z[:, c * GROUP_COLS:(c + 1) * GROUP_COLS]
        ms = _dot((zc * zc).astype(BF16), ones_ref[...]) * (1.0 / HEAD_DIM)
        outs.append(zc * lax.rsqrt(ms + EPS))
    return outs[0] if len(outs) == 1 else jnp.concatenate(outs, axis=-1)


def _mixer_in_kernel(x_ref, mod_ref, gain_ref, w_in_ref, pool_w_ref, pool_s_ref,
                     qg_ref, kg_ref, vng_ref, sgw_ref, sgb_ref, og_ref, ones_ref,
                     *refs, seq_len, n_alias):
    oa_ref, q_ref, k_ref, vt_ref, oc_ref, *kv_refs = refs[n_alias:]
    m = mod_ref[...]
    og = og_ref[...]
    gc = og[:, POOL_WIDTH + NA_WIDTH:]
    lane_grp = lax.broadcasted_iota(jnp.int32, (CHUNK, SG_WIDTH), 1) >> 6
    sgw = sgw_ref[...]
    sgb = sgb_ref[...]
    splits = (0, POOL_WIDTH, POOL_WIDTH + NA_WIDTH, POOL_WIDTH + 2 * NA_WIDTH,
              POOL_WIDTH + 3 * NA_WIDTH, POOL_WIDTH + 3 * NA_WIDTH + SG_WIDTH, w_in_ref.shape[1])

    def pool_out(rows, za):
        pooled = _pool_mixer(za, seq_len)
        a_out = _dot(pooled.astype(BF16), pool_w_ref[...]) * pool_s_ref[...]
        oa_ref[rows, :] = _rms(a_out, og[:, 0:POOL_WIDTH]).astype(BF16)

    starts = [sum(MIX_TILES[:i]) for i in range(len(MIX_TILES))]
    tiles = [slice(r, r + n) for r, n in zip(starts, MIX_TILES)]
    order = (5, 4, 0, 1, 2, 3)
    projected = []
    for rows in tiles:
        h = _rms_mod(x_ref[rows, :], gain_ref[1:2, :], m[3:4], m[4:5]).astype(BF16)
        z = {i: _dot(h, w_in_ref[:, splits[i]:splits[i + 1]]) for i in order}
        projected.append([z[i] for i in range(6)])

    for rows, (za, zq, zk, zv, zu, zsv) in zip(tiles, projected):
        u = jax.nn.gelu(zu)
        vg = (_group_rms(jax.nn.gelu(zsv), ones_ref) * vng_ref[...]).astype(BF16)
        for r0 in range(0, rows.stop - rows.start, CHUNK):
            full = _dot(sgw, vg[r0:r0 + CHUNK, :])
            sp = full[0:CHUNK]
            for gi in range(1, 4):
                sp = jnp.where(lane_grp == gi, full[gi * CHUNK:(gi + 1) * CHUNK], sp)
            c_out = u[r0:r0 + CHUNK, :] * (sp + sgb)
            oc_ref[rows.start + r0:rows.start + r0 + CHUNK, :] = _rms(c_out, gc).astype(BF16)

        if seq_len <= min(MIX_TILES):
            pool_out(rows, za)

        q_ref[rows, :] = (_group_rms(zq, ones_ref) * (qg_ref[...] * QUERY_SCALE)).astype(BF16)
        k = _group_rms(zk, ones_ref) * kg_ref[...]
        k_ref[rows, :] = k.astype(BF16)
        v_t = zv.T
        vt_ref[:, rows] = v_t.astype(BF16)
        for val_t, ref in zip((k.T, v_t) if kv_refs else (), kv_refs):
            for s in range((rows.stop - rows.start) // SEQ):
                for hd in range(NA_HEADS):
                    ref[rows.start // SEQ + s, hd] = val_t[hd * HEAD_DIM:(hd + 1) * HEAD_DIM,
                                                           s * SEQ:(s + 1) * SEQ]

    if seq_len > min(MIX_TILES):
        pool_out(slice(0, GROUP), jnp.concatenate([z[0] for z in projected], axis=0))


def _mixer_in(x, mods, norm_g, w_in, pool_w, pool_s, qg, kg, vng, sgw, sgb, og, ones,
              *, layer, row0, per_group, seq_len, emit_kv, prev_kv=()):
    kern = functools.partial(_mixer_in_kernel, seq_len=seq_len, n_alias=len(prev_kv))
    tok = lambda w, dt: jax.ShapeDtypeStruct((N_GROUPS, GROUP, w), dt)
    tok_spec = lambda w: pl.BlockSpec((None, GROUP, w), lambda g: (g, 0, 0))
    lay = lambda *shape: _resident((None,) + shape, lambda g: (layer,) + (0,) * len(shape))
    n_seq = GROUP // SEQ
    kv_shape = jax.ShapeDtypeStruct((N_GROUPS * n_seq, DEPTH, NA_HEADS, HEAD_DIM, SEQ), F32)
    kv_spec = pl.BlockSpec((n_seq, None, NA_HEADS, HEAD_DIM, SEQ), lambda g: (g, layer, 0, 0, 0))
    kv_shapes = (kv_shape, kv_shape) if emit_kv else ()
    kv_specs = (kv_spec, kv_spec) if emit_kv else ()
    n_in = 13
    aliases = {n_in + i: 5 + i for i in range(len(prev_kv))}
    return pl.pallas_call(
        kern,
        out_shape=(tok(POOL_WIDTH, BF16), tok(NA_WIDTH, BF16), tok(NA_WIDTH, BF16),
                   jax.ShapeDtypeStruct((N_GROUPS, NA_WIDTH, GROUP), BF16),
                   tok(SG_WIDTH, BF16)) + kv_shapes,
        grid=(N_GROUPS,),
        in_specs=[
            tok_spec(D_MODEL),
            _mod_spec(layer, row0, per_group),
            lay(3, D_MODEL),
            _resident(w_in.shape, lambda g: (0, 0)),
            lay(POOL_WIDTH, POOL_WIDTH),
            lay(1, POOL_WIDTH),
            lay(1, NA_WIDTH),
            lay(1, NA_WIDTH),
            lay(1, SG_WIDTH),
            lay(4 * CHUNK, CHUNK),
            lay(CHUNK, SG_WIDTH),
            lay(1, D_MODEL),
            _resident((GROUP_COLS, GROUP_COLS), lambda g: (0, 0)),
        ] + [pl.BlockSpec(memory_space=pl.ANY)] * len(prev_kv),
        out_specs=(tok_spec(POOL_WIDTH), tok_spec(NA_WIDTH), tok_spec(NA_WIDTH),
                   pl.BlockSpec((None, NA_WIDTH, GROUP), lambda g: (g, 0, 0)),
                   tok_spec(SG_WIDTH)) + kv_specs,
        input_output_aliases=aliases,
        compiler_params=_params(("arbitrary",)),
        name="mixer_in",
    )(x, mods, norm_g, w_in, pool_w, pool_s, qg, kg, vng, sgw, sgb, og, ones, *prev_kv)


def _first_head(shape, axis):
    return lax.broadcasted_iota(jnp.int32, shape, axis) < HEAD_DIM


def _head_queries(q):
    first = _first_head(q.shape, 1)
    zero = jnp.zeros_like(q)
    return jnp.where(first, q, zero), jnp.where(first, zero, q)


def _head_values_t(vt):
    first = _first_head(vt.shape, 0)
    one = jnp.ones_like(vt)
    return jnp.where(first, vt, one), jnp.where(first, one, vt)


def _softmax_pv_t(scores_t, values_t):
    mx = scores_t[0].max(axis=0, keepdims=True)
    for s in scores_t[1:]:
        mx = jnp.maximum(mx, s.max(axis=0, keepdims=True))
    out = 0.0
    for s, vt in zip(scores_t, values_t):
        out = out + _dot(vt, jnp.exp2(s - mx).astype(BF16))
    return out


def _merge_heads_t(o0, o1):
    first = _first_head(o0.shape, 0)
    num = jnp.where(first, o0, o1)
    den = pltpu.roll(jnp.where(first, o1, o0), HEAD_DIM, 0)
    return (num / den).T.astype(BF16)


def _ctx_attn_jobs(q_ref, k_ref, vt_ref, o_ref):
    def job(rows, lanes):
        def start():
            k = k_ref[rows, lanes]
            q0, q1 = _head_queries(q_ref[rows, lanes])
            s0, s1 = _dot_nt(k, q0), _dot_nt(k, q1)

            def finish():
                v0, v1 = _head_values_t(vt_ref[lanes, rows])
                o_ref[rows, lanes] = _merge_heads_t(_softmax_pv_t([s0], [v0]),
                                                    _softmax_pv_t([s1], [v1]))
            return finish
        return start

    return [job(slice(s, s + SEQ), slice(p, p + PAIR_W))
            for p in range(0, q_ref.shape[1], PAIR_W) for s in range(0, GROUP, SEQ)]


def _side_spec():
    return pl.BlockSpec((None, GROUP, SIDE_PAIRS * PAIR_W), lambda t, g: (g, 0, t))


def _side_spec_t():
    return pl.BlockSpec((None, SIDE_PAIRS * PAIR_W, GROUP), lambda t, g: (g, t, 0))


def _ctx_attn_side(q, k, vt):
    spec = _side_spec()
    return (_ctx_attn_jobs, (q, k, vt), [spec, spec, _side_spec_t()],
            [jax.ShapeDtypeStruct(q.shape, BF16)], [spec])


NA_QROWS = 4


def _na_plan():
    blocks, tables, offsets, width = [], [], [], 0
    for r0 in range(0, GRID_ROWS, NA_QROWS):
        rs = range(r0, r0 + NA_QROWS)
        starts = [min(max(r - NA_ROWS // 2, 0), GRID_ROWS - NA_ROWS) for r in rs]
        nk = max(starts) + NA_ROWS - min(starts)
        nk += nk % 2
        k0 = min(min(starts), GRID_ROWS - nk)
        table = (nk, tuple((k0 - r + NA_ROWS - 1, s - k0, s - k0 + NA_ROWS)
                           for r, s in zip(rs, starts)))
        assert all(d >= 0 and d + nk <= N_RPB_ROWS + 1 for d, _, _ in table[1])
        if table not in tables:
            tables.append(table)
            offsets.append(width)
            width += nk * GRID_W
        blocks.append((r0, k0, nk, offsets[tables.index(table)]))
    return tuple(blocks), tuple(tables), width


def _bias_kernel(rexp_ref, o_ref, *, tables):
    shape = (GRID_W, 16 * GRID_W)
    cq = lax.broadcasted_iota(jnp.int32, shape, 0)
    ck = lax.broadcasted_iota(jnp.int32, shape, 1) & (GRID_W - 1)
    dc = jnp.clip(ck - cq, -(NA_COLS - 1), NA_COLS - 1) + (NA_COLS - 1)
    col_start = jnp.clip(cq - NA_COLS // 2, 0, GRID_W - NA_COLS)
    ok = (ck >= col_start) & (ck < col_start + NA_COLS)
    table = jnp.zeros(shape, F32)
    for j in range(N_RPB_COLS):
        table = jnp.where(dc == j, jnp.broadcast_to(rexp_ref[j:j + 1, :], shape), table)
    table = jnp.where(ok, table * LOG2E, MASKED)
    col0 = 0
    for nk, rows in tables:
        width = nk * GRID_W
        krow = lax.broadcasted_iota(jnp.int32, (GRID_W, width), 1) >> 6
        slabs = []
        for d, lo, hi in rows:
            slab = table[:, d * GRID_W:d * GRID_W + width]
            if lo > 0 or hi < nk:
                slab = jnp.where((krow >= lo) & (krow < hi), slab, MASKED)
            slabs.append(slab)
        o_ref[col0:col0 + width, :] = jnp.concatenate(slabs, axis=0).T
        col0 += width


def _na_attn_jobs(q_ref, k_ref, vt_ref, ck_ref, cv_ref, bias_ref, o_ref, *, blocks):
    pair_cache = {}

    def cached_pair(ref, p):
        key = (id(ref), p)
        if key not in pair_cache:
            pair_cache[key] = jnp.concatenate([ref[2 * p], ref[2 * p + 1]], axis=0).astype(BF16)
        return pair_cache[key]

    def cached_values(p):
        if ("values", p) not in pair_cache:
            pair_cache["values", p] = _head_values_t(cached_pair(cv_ref, p))
        return pair_cache["values", p]

    def job(p, r0, k0, nk, col0):
        lanes = slice(p * PAIR_W, (p + 1) * PAIR_W)
        h0 = 2 * p
        q_rows = slice(r0 * GRID_W, (r0 + NA_QROWS) * GRID_W)
        k_rows = slice(k0 * GRID_W, (k0 + nk) * GRID_W)

        def start():
            kc_t = cached_pair(ck_ref, p)
            k_loc = k_ref[k_rows, lanes]
            scores = []
            for e, qe in enumerate(_head_queries(q_ref[q_rows, lanes])):
                s_loc = _dot_nt(k_loc, qe) + bias_ref[h0 + e, col0:col0 + nk * GRID_W, :]
                scores.append([s_loc, _dot(qe, kc_t).T])

            def finish():
                vc_t = cached_values(p)
                v_loc = _head_values_t(vt_ref[lanes, k_rows])
                o_ref[q_rows, lanes] = _merge_heads_t(
                    *[_softmax_pv_t(scores[e], [v_loc[e], vc_t[e]]) for e in range(2)])
            return finish
        return start

    return [job(p, *blk) for p in range(q_ref.shape[1] // PAIR_W) for blk in blocks]


def _na_attn_side(q, k, vt, cache_kt, cache_vt, bias, blocks, *, layer):
    spec = _side_spec()
    n_heads = 2 * SIDE_PAIRS
    cspec = pl.BlockSpec((None, None, n_heads, HEAD_DIM, PAST_LEN),
                         lambda t, g: (g, layer, t, 0, 0))
    bspec = _resident((None, n_heads) + bias.shape[2:], lambda t, g: (layer, t, 0, 0))
    return (functools.partial(_na_attn_jobs, blocks=blocks),
            (q, k, vt, cache_kt, cache_vt, bias),
            [spec, spec, _side_spec_t(), cspec, cspec, bspec],
            [jax.ShapeDtypeStruct(q.shape, BF16)], [spec])


def _mixer_out_ffn_kernel(x_ref, oa_ref, ob_ref, oc_ref, mod_ref, gain_ref, og_ref, wo_ref,
                          wg_ref, wu_ref, wd_ref, o_ref, *, side_jobs=()):
    m = mod_ref[...]
    gb = og_ref[:, POOL_WIDTH:POOL_WIDTH + NA_WIDTH]
    c0 = POOL_WIDTH
    c1 = POOL_WIDTH + NA_WIDTH
    ob = _rms(ob_ref[...].astype(F32), gb).astype(BF16)
    o = (_dot(oa_ref[...], wo_ref[0:c0, :]) + _dot(ob, wo_ref[c0:c1, :])
         + _dot(oc_ref[...], wo_ref[c1:, :]))
    x = x_ref[...] + m[5:6] * o
    o_ref[...] = _ffn_tile(x, gain_ref[2:3, :], m[6:7], m[7:8], m[8:9], wg_ref, wu_ref, wd_ref,
                           side_jobs)


def _mixer_out_ffn(x, oa, ob, oc, mods, norm_g, og, wo, wg, wu, wd, *, layer, row0, per_group,
                   sides=()):
    specs = [
        _tile_spec(D_MODEL), _tile_spec(POOL_WIDTH), _tile_spec(NA_WIDTH), _tile_spec(SG_WIDTH),
        _mod_spec(layer, row0, per_group, group_axis=1),
        _resident((None, 3, D_MODEL), lambda t, g: (layer, 0, 0)),
        _resident((None, 1, D_MODEL), lambda t, g: (layer, 0, 0)),
        _whole(wo), _whole(wg), _whole(wu), _whole(wd),
    ]
    return _tile_call("mixer_out_ffn", _mixer_out_ffn_kernel,
                      (x, oa, ob, oc, mods, norm_g, og, wo, wg, wu, wd), specs, x.shape, sides)


def _block_diag(w):
    depth, groups, d, _ = w.shape
    same = jnp.arange(groups)[:, None] == jnp.arange(groups)[None, :]
    blocks = jnp.where(same[None, :, None, :, None], w[:, :, :, None, :], 0.0)
    return blocks.reshape(depth, groups * d, groups * d)


def kernel(x_prompt, x_sample, cache_k, cache_v, c, c_ctx, ada_w, ada_b, norm_g, ffn_w_gate,
           ffn_w_up, ffn_w_down, w_in, pool_w, pool_scale, q_norm_g, k_norm_g, na_rpb,
           sg_vnorm_g, sg_w, sg_b, out_norm_g, w_out):
    cvec = jnp.concatenate([c_ctx[None], c])

    big = (ffn_w_gate, ffn_w_up, ffn_w_down, w_in, w_out)
    pool_wb = _block_diag(pool_w).astype(BF16)
    pool_s = pool_scale.reshape(DEPTH, 1, POOL_WIDTH)
    qg = jnp.tile(q_norm_g, (1, NA_HEADS)).reshape(DEPTH, 1, NA_WIDTH)
    kg = jnp.tile(k_norm_g, (1, NA_HEADS)).reshape(DEPTH, 1, NA_WIDTH)
    vng = sg_vnorm_g.reshape(DEPTH, 1, SG_WIDTH)
    sgw = sg_w.reshape(DEPTH, 4 * CHUNK, CHUNK).astype(BF16)
    sgb = jnp.repeat(jnp.swapaxes(sg_b, 1, 2), HEAD_DIM, axis=-1)
    og = out_norm_g.reshape(DEPTH, 1, D_MODEL)
    lane = jnp.arange(GROUP_COLS) // HEAD_DIM
    ones = (lane[:, None] == lane[None, :]).astype(BF16)
    rexp = jnp.repeat(jnp.swapaxes(na_rpb, 2, 3), GRID_W, axis=-1)
    rexp = jnp.pad(rexp, ((0, 0), (0, 0), (0, 1), (0, GRID_W)))
    na_blocks, na_tables, na_width = _na_plan()
    cache_kt = jnp.swapaxes(cache_k, 3, 4)
    cache_vt = jnp.swapaxes(cache_v, 3, 4)

    xp = x_prompt.reshape(N_GROUPS, GROUP, D_MODEL)
    xs = x_sample.reshape(N_GROUPS, GROUP, D_MODEL)
    ctx = dict(row0=0, per_group=0)
    lat = dict(row0=1, per_group=1)
    new_kv = ()

    gate, up, down, w_in_m, w_out_m = (w.reshape(-1, w.shape[-1]) for w in big)

    def half_parts(layer, half):
        i = 2 * layer + half
        return [(gate, i * D_MODEL, D_MODEL), (up, i * D_MODEL, D_MODEL), (down, i * D_FF, D_FF)]

    def rest_parts(layer):
        return half_parts(layer, 1) + [(w_in_m, layer * D_MODEL, D_MODEL),
                                       (w_out_m, layer * D_MODEL, D_MODEL)]

    mods, bias, first_half = _prepare(cvec, ada_w, ada_b, rexp, na_tables, na_width,
                                      half_parts(0, 0))
    for l in range(DEPTH):
        last = l == DEPTH - 1
        if l == 0:
            xp, *rest = _ffn(xp, mods, norm_g, *first_half, layer=l, half=0, **ctx,
                             sides=[_to_bf16_side(rest_parts(0))])
        else:
            xp, = _ffn(xp, mods, norm_g, *first_half, layer=l, half=0, **ctx)
        second_half, (w_in_b, wo) = rest[:3], rest[3:]
        mix = (mods, norm_g, w_in_b, pool_wb, pool_s, qg, kg, vng, sgw, sgb, og, ones)
        oa_p, q, k, v, oc_p, *new_kv = _mixer_in(xp, *mix, layer=l, seq_len=SEQ, emit_kv=True,
                                                 prev_kv=new_kv, **ctx)
        xs, ob_p = _ffn(xs, mods, norm_g, *first_half, layer=l, half=0, **lat,
                        sides=[_ctx_attn_side(q, k, v)])
        oa_s, q, k, v, oc_s = _mixer_in(xs, *mix, layer=l, seq_len=GROUP, emit_kv=False, **lat)
        xp, ob_s = _mixer_out_ffn(
            xp, oa_p, ob_p, oc_p, mods, norm_g, og, wo, *second_half, layer=l, **ctx,
            sides=[_na_attn_side(q, k, v, cache_kt, cache_vt, bias, na_blocks, layer=l)])
        sides = [] if last else [_to_bf16_side(half_parts(l + 1, 0) + rest_parts(l + 1))]
        xs, *cast = _mixer_out_ffn(xs, oa_s, ob_s, oc_s, mods, norm_g, og, wo, *second_half,
                                   layer=l, **lat, sides=sides)
        if not last:
            first_half, rest = cast[:3], cast[3:]

    return (xp.reshape(x_prompt.shape), xs.reshape(x_sample.shape),
            jnp.swapaxes(new_kv[0], 3, 4), jnp.swapaxes(new_kv[1], 3, 4))
```

```python
import functools

import jax
import jax.numpy as jnp
from jax import lax
from jax.experimental import pallas as pl
from jax.experimental.pallas import tpu as pltpu

F32 = jnp.float32
BF16 = jnp.bfloat16

D_MODEL = 1024
D_FF = 2816
N_MOD = 9
DEPTH = 2
GROUP = 1024
N_GROUPS = 8
SEQ = 256
GRID_W = 64
GRID_ROWS = GROUP // GRID_W
POOL_WIDTH = 256
NA_WIDTH = 512
SG_WIDTH = 256
HEAD_DIM = 64
NA_HEADS = 8
HEAD_PAIRS = NA_HEADS // 2
PAIR_W = 2 * HEAD_DIM
NA_ROWS = 8
NA_COLS = 16
N_RPB_ROWS = 2 * NA_ROWS - 1
N_RPB_COLS = 2 * NA_COLS - 1
CHUNK = 128
PAST_LEN = 256
EPS = 1e-6
MASKED = -1e30
LOG2E = 1.4426950408889634
QUERY_SCALE = HEAD_DIM ** -0.5 * LOG2E

FFN_TM = 512
FFN_FC = 256
MIX_TILES = (512, 512)
VMEM_LIMIT = 56 * 1024 * 1024


def _dot(a, b):
    return jnp.dot(a, b, preferred_element_type=F32)


def _dot_nt(a, b):
    return lax.dot_general(a, b, (((1,), (1,)), ((), ())), preferred_element_type=F32)


def _rms(x, gain):
    ms = jnp.mean(x * x, axis=-1, keepdims=True)
    return x * lax.rsqrt(ms + EPS) * gain


def _rms_mod(x, gain, shift, scale):
    ms = jnp.mean(x * x, axis=-1, keepdims=True)
    return x * lax.rsqrt(ms + EPS) * (gain * (1.0 + scale)) + shift


def _params(sem):
    return pltpu.CompilerParams(dimension_semantics=sem, vmem_limit_bytes=VMEM_LIMIT)


def _resident(shape, index_map):
    return pl.BlockSpec(shape, index_map, pipeline_mode=pl.Buffered(1))


ADALN_TN = 1024


def _adaln_kernel(c_ref, w_ref, b_ref, o_ref):
    s = jax.nn.silu(c_ref[...]).astype(BF16)
    o_ref[...] = _dot(s, w_ref[...].astype(BF16)) + b_ref[...]


ADALN_TILES = N_MOD * D_MODEL // ADALN_TN
PREP_STEPS = DEPTH * ADALN_TILES
PREP_PARTS = DEPTH * NA_HEADS


def _prep_kernel(c_ref, aw_ref, ab_ref, rexp_ref, *refs, tables, n_cast):
    mods_ref, bias_ref = refs[n_cast:n_cast + 2]
    _adaln_kernel(c_ref, aw_ref, ab_ref, mods_ref)
    _bias_kernel(rexp_ref, bias_ref, tables=tables)
    for src, dst in zip(refs[:n_cast], refs[n_cast + 2:]):
        dst[...] = src[...].astype(BF16)


def _cast_specs(parts, n_steps, step_of):
    arrays, in_specs, out_shapes, out_specs = [], [], [], []
    for mat, first, rows in parts:
        height = rows // n_steps
        base = first // height
        arrays.append(mat)
        in_specs.append(pl.BlockSpec((height, mat.shape[1]),
                                     lambda *idx, base=base: (base + step_of(*idx), 0)))
        out_shapes.append(jax.ShapeDtypeStruct((rows, mat.shape[1]), BF16))
        out_specs.append(pl.BlockSpec((height, mat.shape[1]), lambda *idx: (step_of(*idx), 0)))
    return arrays, in_specs, out_shapes, out_specs


def _prepare(cvec, ada_w, ada_b, rexp, tables, width, cast_parts):
    rows = cvec.shape[0]
    part = lambda s: jnp.minimum(s, PREP_PARTS - 1)
    tile = lambda s: (s // ADALN_TILES, 0, s % ADALN_TILES)
    head = lambda s: (part(s) // NA_HEADS, part(s) % NA_HEADS, 0, 0)
    mats, cast_in, cast_shapes, cast_out = _cast_specs(cast_parts, PREP_PARTS, part)
    outs = pl.pallas_call(
        functools.partial(_prep_kernel, tables=tables, n_cast=len(mats)),
        out_shape=[jax.ShapeDtypeStruct((DEPTH, rows, N_MOD * D_MODEL), F32),
                   jax.ShapeDtypeStruct((DEPTH, NA_HEADS, width, NA_QROWS * GRID_W), BF16)]
        + cast_shapes,
        grid=(PREP_STEPS,),
        in_specs=[
            pl.BlockSpec((rows, D_MODEL), lambda s: (0, 0)),
            pl.BlockSpec((None, D_MODEL, ADALN_TN), tile),
            pl.BlockSpec((None, 1, ADALN_TN), tile),
            pl.BlockSpec((None, None, N_RPB_COLS + 1, 16 * GRID_W), head),
        ] + cast_in,
        out_specs=[
            pl.BlockSpec((None, rows, ADALN_TN), tile),
            pl.BlockSpec((None, None, width, NA_QROWS * GRID_W), head),
        ] + cast_out,
        compiler_params=_params(("arbitrary",)),
        name="prepare",
    )(cvec, ada_w, ada_b.reshape(DEPTH, 1, N_MOD * D_MODEL), rexp, *mats)
    return outs[0].reshape(DEPTH, rows, N_MOD, D_MODEL), outs[1], outs[2:]


def _ffn_tile(x, gain, shift, scale, gate, wg_ref, wu_ref, wd_ref, side_jobs=()):
    h = _rms_mod(x, gain, shift, scale).astype(BF16)
    acc = jnp.zeros(x.shape, F32)
    n_chunks = D_FF // FFN_FC
    per_chunk = -(-len(side_jobs) // n_chunks)
    for c in range(n_chunks):
        finishers = [job() for job in side_jobs[c * per_chunk:(c + 1) * per_chunk]]
        lo = c * FFN_FC
        g = _dot(h, wg_ref[:, lo:lo + FFN_FC])
        u = _dot(h, wu_ref[:, lo:lo + FFN_FC])
        a = (g * jax.nn.sigmoid(g) * u).astype(BF16)
        acc = acc + _dot(a, wd_ref[lo:lo + FFN_FC, :])
        for finish in finishers:
            finish()
    return x + 0.5 * gate * acc


def _ffn_kernel(x_ref, mod_ref, gain_ref, wg_ref, wu_ref, wd_ref, o_ref, *, mod0, gain_row,
                side_jobs=()):
    m = mod_ref[...]
    o_ref[...] = _ffn_tile(x_ref[...], gain_ref[gain_row:gain_row + 1, :],
                           m[mod0:mod0 + 1], m[mod0 + 1:mod0 + 2], m[mod0 + 2:mod0 + 3],
                           wg_ref, wu_ref, wd_ref, side_jobs)


def _mod_spec(layer, row0, per_group, group_axis=0):
    return pl.BlockSpec((None, None, N_MOD, D_MODEL),
                        lambda *idx: (layer, row0 + idx[group_axis] * per_group, 0, 0))


def _whole(array):
    return _resident(array.shape, lambda *_: (0, 0))


N_TILES = GROUP // FFN_TM
N_STEPS = N_TILES * N_GROUPS
SIDE_PAIRS = HEAD_PAIRS // N_TILES


def _tile_spec(width):
    return pl.BlockSpec((None, FFN_TM, width), lambda t, g: (g, t, 0))


def _step(t, g):
    return t * N_GROUPS + g


def _one_stage(fn):
    def start():
        fn()
        return lambda: None
    return start


def _joint_kernel(*refs, main, n_main, sides):
    outs = refs[n_main + sum(n_in for _, n_in, _ in sides):]
    jobs = []
    i, o = n_main, 1
    for jobs_fn, n_in, n_out in sides:
        jobs += jobs_fn(*refs[i:i + n_in], *outs[o:o + n_out])
        i, o = i + n_in, o + n_out
    main(*refs[:n_main], outs[0], side_jobs=jobs)


def _tile_call(name, main, main_args, main_specs, x_shape, sides=()):
    kern = functools.partial(
        _joint_kernel, main=main, n_main=len(main_args),
        sides=tuple((fn, len(args), len(shapes)) for fn, args, _, shapes, _ in sides))
    outs = pl.pallas_call(
        kern,
        out_shape=[jax.ShapeDtypeStruct(x_shape, F32)] + [s for side in sides for s in side[3]],
        grid=(N_TILES, N_GROUPS),
        in_specs=list(main_specs) + [s for side in sides for s in side[2]],
        out_specs=[_tile_spec(D_MODEL)] + [s for side in sides for s in side[4]],
        compiler_params=_params(("arbitrary", "arbitrary")),
        name=name,
    )(*main_args, *[a for side in sides for a in side[1]])
    return outs


def _ffn(x, mods, norm_g, wg, wu, wd, *, layer, half, row0, per_group, sides=()):
    kern = functools.partial(_ffn_kernel, mod0=6 * half, gain_row=2 * half)
    specs = [
        _tile_spec(D_MODEL),
        _mod_spec(layer, row0, per_group, group_axis=1),
        _resident((None, 3, D_MODEL), lambda t, g: (layer, 0, 0)),
        _whole(wg), _whole(wu), _whole(wd),
    ]
    return _tile_call("ffn", kern, (x, mods, norm_g, wg, wu, wd), specs, x.shape, sides)


def _to_bf16_side(cast_parts):
    mats, in_specs, out_shapes, out_specs = _cast_specs(cast_parts, N_STEPS, _step)

    def jobs(*refs):
        def cast(src, dst):
            def run():
                dst[...] = src[...].astype(BF16)
            return _one_stage(run)
        return [cast(src, dst) for src, dst in zip(refs[:len(mats)], refs[len(mats):])]

    return jobs, mats, in_specs, out_shapes, out_specs


LANES = 128
GROUP_COLS = 256


def _pool_column(a, lane0, seq_len):
    t = a.shape[0]
    pos = lax.broadcasted_iota(jnp.int32, a.shape, 0) & (seq_len - 1)
    lane = lax.broadcasted_iota(jnp.int32, (1, LANES), 1) + lane0
    half = jnp.left_shift(1, lane >> 6)
    max_half = 1 << ((lane0 + LANES - 1) // HEAD_DIM)
    right = a
    left = a
    k = 1
    while 2 * k <= max_half:
        grow = half >= 2 * k
        right = right + jnp.where(pos < jnp.where(grow, seq_len - k, 0),
                                  pltpu.roll(right, t - k, 0), 0.0)
        left = left + jnp.where(pos >= jnp.where(grow, k, seq_len),
                                pltpu.roll(left, k, 0), 0.0)
        k *= 2
    window = right + jnp.where(pos >= 1, pltpu.roll(left, 1, 0), 0.0)
    cnt = jnp.minimum(pos + half, seq_len) - jnp.maximum(pos - half, 0)
    return window / cnt.astype(F32) - a


def _pool_mixer(a, seq_len):
    cols = [_pool_column(a[:, c:c + LANES], c, seq_len) for c in range(0, a.shape[1], LANES)]
    return jnp.concatenate(cols, axis=-1)


def _group_rms(z, ones_ref):
    outs = []
    for c in range(z.shape[1] // GROUP_COLS):
        zc = z[:, c * GROUP_COLS:(c + 1) * GROUP_COLS]
        ms = _dot((zc * zc).astype(BF16), ones_ref[...]) * (1.0 / HEAD_DIM)
        outs.append(zc * lax.rsqrt(ms + EPS))
    return outs[0] if len(outs) == 1 else jnp.concatenate(outs, axis=-1)


def _mixer_in_kernel(x_ref, mod_ref, gain_ref, w_in_ref, pool_w_ref, pool_s_ref,
                     qg_ref, kg_ref, vng_ref, sgw_ref, sgb_ref, og_ref, ones_ref,
                     *refs, seq_len, n_alias):
    oa_ref, q_ref, k_ref, vt_ref, oc_ref, *kv_refs = refs[n_alias:]
    m = mod_ref[...]
    og = og_ref[...]
    gc = og[:, POOL_WIDTH + NA_WIDTH:]
    lane_grp = lax.broadcasted_iota(jnp.int32, (CHUNK, SG_WIDTH), 1) >> 6
    sgw = sgw_ref[...]
    sgb = sgb_ref[...]
    splits = (0, POOL_WIDTH, POOL_WIDTH + NA_WIDTH, POOL_WIDTH + 2 * NA_WIDTH,
              POOL_WIDTH + 3 * NA_WIDTH, POOL_WIDTH + 3 * NA_WIDTH + SG_WIDTH, w_in_ref.shape[1])

    def pool_out(rows, za):
        pooled = _pool_mixer(za, seq_len)
        a_out = _dot(pooled.astype(BF16), pool_w_ref[...]) * pool_s_ref[...]
        oa_ref[rows, :] = _rms(a_out, og[:, 0:POOL_WIDTH]).astype(BF16)

    starts = [sum(MIX_TILES[:i]) for i in range(len(MIX_TILES))]
    tiles = [slice(r, r + n) for r, n in zip(starts, MIX_TILES)]
    order = (5, 4, 0, 1, 2, 3)
    projected = []
    for rows in tiles:
        h = _rms_mod(x_ref[rows, :], gain_ref[1:2, :], m[3:4], m[4:5]).astype(BF16)
        z = {i: _dot(h, w_in_ref[:, splits[i]:splits[i + 1]]) for i in order}
        projected.append([z[i] for i in range(6)])

    for rows, (za, zq, zk, zv, zu, zsv) in zip(tiles, projected):
        u = jax.nn.gelu(zu)
        vg = (_group_rms(jax.nn.gelu(zsv), ones_ref) * vng_ref[...]).astype(BF16)
        for r0 in range(0, rows.stop - rows.start, CHUNK):
            full = _dot(sgw, vg[r0:r0 + CHUNK, :])
            sp = full[0:CHUNK]
            for gi in range(1, 4):
                sp = jnp.where(lane_grp == gi, full[gi * CHUNK:(gi + 1) * CHUNK], sp)
            c_out = u[r0:r0 + CHUNK, :] * (sp + sgb)
            oc_ref[rows.start + r0:rows.start + r0 + CHUNK, :] = _rms(c_out, gc).astype(BF16)

        if seq_len <= min(MIX_TILES):
            pool_out(rows, za)

        q_ref[rows, :] = (_group_rms(zq, ones_ref) * (qg_ref[...] * QUERY_SCALE)).astype(BF16)
        k = _group_rms(zk, ones_ref) * kg_ref[...]
        k_ref[rows, :] = k.astype(BF16)
        v_t = zv.T
        vt_ref[:, rows] = v_t.astype(BF16)
        for val_t, ref in zip((k.T, v_t) if kv_refs else (), kv_refs):
            for s in range((rows.stop - rows.start) // SEQ):
                for hd in range(NA_HEADS):
                    ref[rows.start // SEQ + s, hd] = val_t[hd * HEAD_DIM:(hd + 1) * HEAD_DIM,
                                                           s * SEQ:(s + 1) * SEQ]

    if seq_len > min(MIX_TILES):
        pool_out(slice(0, GROUP), jnp.concatenate([z[0] for z in projected], axis=0))


def _mixer_in(x, mods, norm_g, w_in, pool_w, pool_s, qg, kg, vng, sgw, sgb, og, ones,
              *, layer, row0, per_group, seq_len, emit_kv, prev_kv=()):
    kern = functools.partial(_mixer_in_kernel, seq_len=seq_len, n_alias=len(prev_kv))
    tok = lambda w, dt: jax.ShapeDtypeStruct((N_GROUPS, GROUP, w), dt)
    tok_spec = lambda w: pl.BlockSpec((None, GROUP, w), lambda g: (g, 0, 0))
    lay = lambda *shape: _resident((None,) + shape, lambda g: (layer,) + (0,) * len(shape))
    n_seq = GROUP // SEQ
    kv_shape = jax.ShapeDtypeStruct((N_GROUPS * n_seq, DEPTH, NA_HEADS, HEAD_DIM, SEQ), F32)
    kv_spec = pl.BlockSpec((n_seq, None, NA_HEADS, HEAD_DIM, SEQ), lambda g: (g, layer, 0, 0, 0))
    kv_shapes = (kv_shape, kv_shape) if emit_kv else ()
    kv_specs = (kv_spec, kv_spec) if emit_kv else ()
    n_in = 13
    aliases = {n_in + i: 5 + i for i in range(len(prev_kv))}
    return pl.pallas_call(
        kern,
        out_shape=(tok(POOL_WIDTH, BF16), tok(NA_WIDTH, BF16), tok(NA_WIDTH, BF16),
                   jax.ShapeDtypeStruct((N_GROUPS, NA_WIDTH, GROUP), BF16),
                   tok(SG_WIDTH, BF16)) + kv_shapes,
        grid=(N_GROUPS,),
        in_specs=[
            tok_spec(D_MODEL),
            _mod_spec(layer, row0, per_group),
            lay(3, D_MODEL),
            _resident(w_in.shape, lambda g: (0, 0)),
            lay(POOL_WIDTH, POOL_WIDTH),
            lay(1, POOL_WIDTH),
            lay(1, NA_WIDTH),
            lay(1, NA_WIDTH),
            lay(1, SG_WIDTH),
            lay(4 * CHUNK, CHUNK),
            lay(CHUNK, SG_WIDTH),
            lay(1, D_MODEL),
            _resident((GROUP_COLS, GROUP_COLS), lambda g: (0, 0)),
        ] + [pl.BlockSpec(memory_space=pl.ANY)] * len(prev_kv),
        out_specs=(tok_spec(POOL_WIDTH), tok_spec(NA_WIDTH), tok_spec(NA_WIDTH),
                   pl.BlockSpec((None, NA_WIDTH, GROUP), lambda g: (g, 0, 0)),
                   tok_spec(SG_WIDTH)) + kv_specs,
        input_output_aliases=aliases,
        compiler_params=_params(("arbitrary",)),
        name="mixer_in",
    )(x, mods, norm_g, w_in, pool_w, pool_s, qg, kg, vng, sgw, sgb, og, ones, *prev_kv)


def _first_head(shape, axis):
    return lax.broadcasted_iota(jnp.int32, shape, axis) < HEAD_DIM


def _head_queries(q):
    first = _first_head(q.shape, 1)
    zero = jnp.zeros_like(q)
    return jnp.where(first, q, zero), jnp.where(first, zero, q)


def _head_values_t(vt):
    first = _first_head(vt.shape, 0)
    one = jnp.ones_like(vt)
    return jnp.where(first, vt, one), jnp.where(first, one, vt)


def _softmax_pv_t(scores_t, values_t):
    mx = scores_t[0].max(axis=0, keepdims=True)
    for s in scores_t[1:]:
        mx = jnp.maximum(mx, s.max(axis=0, keepdims=True))
    out = 0.0
    for s, vt in zip(scores_t, values_t):
        out = out + _dot(vt, jnp.exp2(s - mx).astype(BF16))
    return out


def _merge_heads_t(o0, o1):
    first = _first_head(o0.shape, 0)
    num = jnp.where(first, o0, o1)
    den = pltpu.roll(jnp.where(first, o1, o0), HEAD_DIM, 0)
    return (num / den).T.astype(BF16)


def _ctx_attn_jobs(q_ref, k_ref, vt_ref, o_ref):
    def job(rows, lanes):
        def start():
            k = k_ref[rows, lanes]
            q0, q1 = _head_queries(q_ref[rows, lanes])
            s0, s1 = _dot_nt(k, q0), _dot_nt(k, q1)

            def finish():
                v0, v1 = _head_values_t(vt_ref[lanes, rows])
                o_ref[rows, lanes] = _merge_heads_t(_softmax_pv_t([s0], [v0]),
                                                    _softmax_pv_t([s1], [v1]))
            return finish
        return start

    return [job(slice(s, s + SEQ), slice(p, p + PAIR_W))
            for p in range(0, q_ref.shape[1], PAIR_W) for s in range(0, GROUP, SEQ)]


def _side_spec():
    return pl.BlockSpec((None, GROUP, SIDE_PAIRS * PAIR_W), lambda t, g: (g, 0, t))


def _side_spec_t():
    return pl.BlockSpec((None, SIDE_PAIRS * PAIR_W, GROUP), lambda t, g: (g, t, 0))


def _ctx_attn_side(q, k, vt):
    spec = _side_spec()
    return (_ctx_attn_jobs, (q, k, vt), [spec, spec, _side_spec_t()],
            [jax.ShapeDtypeStruct(q.shape, BF16)], [spec])


NA_QROWS = 4


def _na_plan():
    blocks, tables, offsets, width = [], [], [], 0
    for r0 in range(0, GRID_ROWS, NA_QROWS):
        rs = range(r0, r0 + NA_QROWS)
        starts = [min(max(r - NA_ROWS // 2, 0), GRID_ROWS - NA_ROWS) for r in rs]
        nk = max(starts) + NA_ROWS - min(starts)
        nk += nk % 2
        k0 = min(min(starts), GRID_ROWS - nk)
        table = (nk, tuple((k0 - r + NA_ROWS - 1, s - k0, s - k0 + NA_ROWS)
                           for r, s in zip(rs, starts)))
        assert all(d >= 0 and d + nk <= N_RPB_ROWS + 1 for d, _, _ in table[1])
        if table not in tables:
            tables.append(table)
            offsets.append(width)
            width += nk * GRID_W
        blocks.append((r0, k0, nk, offsets[tables.index(table)]))
    return tuple(blocks), tuple(tables), width


def _bias_kernel(rexp_ref, o_ref, *, tables):
    shape = (GRID_W, 16 * GRID_W)
    cq = lax.broadcasted_iota(jnp.int32, shape, 0)
    ck = lax.broadcasted_iota(jnp.int32, shape, 1) & (GRID_W - 1)
    dc = jnp.clip(ck - cq, -(NA_COLS - 1), NA_COLS - 1) + (NA_COLS - 1)
    col_start = jnp.clip(cq - NA_COLS // 2, 0, GRID_W - NA_COLS)
    ok = (ck >= col_start) & (ck < col_start + NA_COLS)
    table = jnp.zeros(shape, F32)
    for j in range(N_RPB_COLS):
        table = jnp.where(dc == j, jnp.broadcast_to(rexp_ref[j:j + 1, :], shape), table)
    table = jnp.where(ok, table * LOG2E, MASKED)
    col0 = 0
    for nk, rows in tables:
        width = nk * GRID_W
        krow = lax.broadcasted_iota(jnp.int32, (GRID_W, width), 1) >> 6
        slabs = []
        for d, lo, hi in rows:
            slab = table[:, d * GRID_W:d * GRID_W + width]
            if lo > 0 or hi < nk:
                slab = jnp.where((krow >= lo) & (krow < hi), slab, MASKED)
            slabs.append(slab)
        o_ref[col0:col0 + width, :] = jnp.concatenate(slabs, axis=0).T.astype(o_ref.dtype)
        col0 += width


def _na_attn_jobs(q_ref, k_ref, vt_ref, ck_ref, cv_ref, bias_ref, o_ref, *, blocks):
    pair_cache = {}

    def cached_pair(ref, p):
        key = (id(ref), p)
        if key not in pair_cache:
            pair_cache[key] = jnp.concatenate([ref[2 * p], ref[2 * p + 1]], axis=0).astype(BF16)
        return pair_cache[key]

    def cached_values(p):
        if ("values", p) not in pair_cache:
            pair_cache["values", p] = _head_values_t(cached_pair(cv_ref, p))
        return pair_cache["values", p]

    def job(p, r0, k0, nk, col0):
        lanes = slice(p * PAIR_W, (p + 1) * PAIR_W)
        h0 = 2 * p
        q_rows = slice(r0 * GRID_W, (r0 + NA_QROWS) * GRID_W)
        k_rows = slice(k0 * GRID_W, (k0 + nk) * GRID_W)

        def start():
            kc_t = cached_pair(ck_ref, p)
            k_loc = k_ref[k_rows, lanes]
            scores = []
            for e, qe in enumerate(_head_queries(q_ref[q_rows, lanes])):
                bias = bias_ref[h0 + e, col0:col0 + nk * GRID_W, :].astype(F32)
                s_loc = _dot_nt(k_loc, qe) + bias
                scores.append([s_loc, _dot(qe, kc_t).T])

            def finish():
                vc_t = cached_values(p)
                v_loc = _head_values_t(vt_ref[lanes, k_rows])
                o_ref[q_rows, lanes] = _merge_heads_t(
                    *[_softmax_pv_t(scores[e], [v_loc[e], vc_t[e]]) for e in range(2)])
            return finish
        return start

    return [job(p, *blk) for p in range(q_ref.shape[1] // PAIR_W) for blk in blocks]


def _na_attn_side(q, k, vt, cache_kt, cache_vt, bias, blocks, *, layer):
    spec = _side_spec()
    n_heads = 2 * SIDE_PAIRS
    cspec = pl.BlockSpec((None, None, n_heads, HEAD_DIM, PAST_LEN),
                         lambda t, g: (g, layer, t, 0, 0))
    bspec = pl.BlockSpec((None, n_heads) + bias.shape[2:], lambda t, g: (layer, t, 0, 0))
    return (functools.partial(_na_attn_jobs, blocks=blocks),
            (q, k, vt, cache_kt, cache_vt, bias),
            [spec, spec, _side_spec_t(), cspec, cspec, bspec],
            [jax.ShapeDtypeStruct(q.shape, BF16)], [spec])


def _mixer_out_ffn_kernel(x_ref, oa_ref, ob_ref, oc_ref, mod_ref, gain_ref, og_ref, wo_ref,
                          wg_ref, wu_ref, wd_ref, o_ref, *, side_jobs=()):
    m = mod_ref[...]
    gb = og_ref[:, POOL_WIDTH:POOL_WIDTH + NA_WIDTH]
    c0 = POOL_WIDTH
    c1 = POOL_WIDTH + NA_WIDTH
    ob = _rms(ob_ref[...].astype(F32), gb).astype(BF16)
    o = (_dot(oa_ref[...], wo_ref[0:c0, :]) + _dot(ob, wo_ref[c0:c1, :])
         + _dot(oc_ref[...], wo_ref[c1:, :]))
    x = x_ref[...] + m[5:6] * o
    o_ref[...] = _ffn_tile(x, gain_ref[2:3, :], m[6:7], m[7:8], m[8:9], wg_ref, wu_ref, wd_ref,
                           side_jobs)


def _mixer_out_ffn(x, oa, ob, oc, mods, norm_g, og, wo, wg, wu, wd, *, layer, row0, per_group,
                   sides=()):
    specs = [
        _tile_spec(D_MODEL), _tile_spec(POOL_WIDTH), _tile_spec(NA_WIDTH), _tile_spec(SG_WIDTH),
        _mod_spec(layer, row0, per_group, group_axis=1),
        _resident((None, 3, D_MODEL), lambda t, g: (layer, 0, 0)),
        _resident((None, 1, D_MODEL), lambda t, g: (layer, 0, 0)),
        _whole(wo), _whole(wg), _whole(wu), _whole(wd),
    ]
    return _tile_call("mixer_out_ffn", _mixer_out_ffn_kernel,
                      (x, oa, ob, oc, mods, norm_g, og, wo, wg, wu, wd), specs, x.shape, sides)


def _block_diag(w):
    depth, groups, d, _ = w.shape
    same = jnp.arange(groups)[:, None] == jnp.arange(groups)[None, :]
    blocks = jnp.where(same[None, :, None, :, None], w[:, :, :, None, :], 0.0)
    return blocks.reshape(depth, groups * d, groups * d)


def kernel(x_prompt, x_sample, cache_k, cache_v, c, c_ctx, ada_w, ada_b, norm_g, ffn_w_gate,
           ffn_w_up, ffn_w_down, w_in, pool_w, pool_scale, q_norm_g, k_norm_g, na_rpb,
           sg_vnorm_g, sg_w, sg_b, out_norm_g, w_out):
    cvec = jnp.concatenate([c_ctx[None], c])

    big = (ffn_w_gate, ffn_w_up, ffn_w_down, w_in, w_out)
    pool_wb = _block_diag(pool_w).astype(BF16)
    pool_s = pool_scale.reshape(DEPTH, 1, POOL_WIDTH)
    qg = jnp.tile(q_norm_g, (1, NA_HEADS)).reshape(DEPTH, 1, NA_WIDTH)
    kg = jnp.tile(k_norm_g, (1, NA_HEADS)).reshape(DEPTH, 1, NA_WIDTH)
    vng = sg_vnorm_g.reshape(DEPTH, 1, SG_WIDTH)
    sgw = sg_w.reshape(DEPTH, 4 * CHUNK, CHUNK).astype(BF16)
    sgb = jnp.repeat(jnp.swapaxes(sg_b, 1, 2), HEAD_DIM, axis=-1)
    og = out_norm_g.reshape(DEPTH, 1, D_MODEL)
    lane = jnp.arange(GROUP_COLS) // HEAD_DIM
    ones = (lane[:, None] == lane[None, :]).astype(BF16)
    rexp = jnp.repeat(jnp.swapaxes(na_rpb, 2, 3), GRID_W, axis=-1)
    rexp = jnp.pad(rexp, ((0, 0), (0, 0), (0, 1), (0, GRID_W)))
    na_blocks, na_tables, na_width = _na_plan()
    cache_kt = jnp.swapaxes(cache_k, 3, 4)
    cache_vt = jnp.swapaxes(cache_v, 3, 4)

    xp = x_prompt.reshape(N_GROUPS, GROUP, D_MODEL)
    xs = x_sample.reshape(N_GROUPS, GROUP, D_MODEL)
    ctx = dict(row0=0, per_group=0)
    lat = dict(row0=1, per_group=1)
    new_kv = ()

    gate, up, down, w_in_m, w_out_m = (w.reshape(-1, w.shape[-1]) for w in big)

    def half_parts(layer, half):
        i = 2 * layer + half
        return [(gate, i * D_MODEL, D_MODEL), (up, i * D_MODEL, D_MODEL), (down, i * D_FF, D_FF)]

    def rest_parts(layer):
        return half_parts(layer, 1) + [(w_in_m, layer * D_MODEL, D_MODEL),
                                       (w_out_m, layer * D_MODEL, D_MODEL)]

    mods, bias, first_half = _prepare(cvec, ada_w, ada_b, rexp, na_tables, na_width,
                                      half_parts(0, 0))
    for l in range(DEPTH):
        last = l == DEPTH - 1
        if l == 0:
            xp, *rest = _ffn(xp, mods, norm_g, *first_half, layer=l, half=0, **ctx,
                             sides=[_to_bf16_side(rest_parts(0))])
        else:
            xp, = _ffn(xp, mods, norm_g, *first_half, layer=l, half=0, **ctx)
        second_half, (w_in_b, wo) = rest[:3], rest[3:]
        mix = (mods, norm_g, w_in_b, pool_wb, pool_s, qg, kg, vng, sgw, sgb, og, ones)
        oa_p, q, k, v, oc_p, *new_kv = _mixer_in(xp, *mix, layer=l, seq_len=SEQ, emit_kv=True,
                                                 prev_kv=new_kv, **ctx)
        xs, ob_p = _ffn(xs, mods, norm_g, *first_half, layer=l, half=0, **lat,
                        sides=[_ctx_attn_side(q, k, v)])
        oa_s, q, k, v, oc_s = _mixer_in(xs, *mix, layer=l, seq_len=GROUP, emit_kv=False, **lat)
        xp, ob_s = _mixer_out_ffn(
            xp, oa_p, ob_p, oc_p, mods, norm_g, og, wo, *second_half, layer=l, **ctx,
            sides=[_na_attn_side(q, k, v, cache_kt, cache_vt, bias, na_blocks, layer=l)])
        sides = [] if last else [_to_bf16_side(half_parts(l + 1, 0) + rest_parts(l + 1))]
        xs, *cast = _mixer_out_ffn(xs, oa_s, ob_s, oc_s, mods, norm_g, og, wo, *second_half,
                                   layer=l, **lat, sides=sides)
        if not last:
            first_half, rest = cast[:3], cast[3:]

    return (xp.reshape(x_prompt.shape), xs.reshape(x_sample.shape),
            jnp.swapaxes(new_kv[0], 3, 4), jnp.swapaxes(new_kv[1], 3, 4))
```

```python
import functools

import jax
import jax.numpy as jnp
from jax import lax
from jax.experimental import pallas as pl
from jax.experimental.pallas import tpu as pltpu

F32 = jnp.float32
BF16 = jnp.bfloat16

D_MODEL = 1024
D_FF = 2816
N_MOD = 9
DEPTH = 2
GROUP = 1024
N_GROUPS = 8
SEQ = 256
GRID_W = 64
GRID_ROWS = GROUP // GRID_W
POOL_WIDTH = 256
NA_WIDTH = 512
SG_WIDTH = 256
HEAD_DIM = 64
NA_HEADS = 8
HEAD_PAIRS = NA_HEADS // 2
PAIR_W = 2 * HEAD_DIM
NA_ROWS = 8
NA_COLS = 16
N_RPB_ROWS = 2 * NA_ROWS - 1
N_RPB_COLS = 2 * NA_COLS - 1
CHUNK = 128
PAST_LEN = 256
EPS = 1e-6
MASKED = -1e30
LOG2E = 1.4426950408889634
QUERY_SCALE = HEAD_DIM ** -0.5 * LOG2E

FFN_TM = 512
FFN_FC = 256
MIX_TILES = (768, 256)
VMEM_LIMIT = 56 * 1024 * 1024


def _dot(a, b):
    return jnp.dot(a, b, preferred_element_type=F32)


def _dot_nt(a, b):
    return lax.dot_general(a, b, (((1,), (1,)), ((), ())), preferred_element_type=F32)


def _rms(x, gain):
    ms = jnp.mean(x * x, axis=-1, keepdims=True)
    return x * lax.rsqrt(ms + EPS) * gain


def _rms_mod(x, gain, shift, scale):
    ms = jnp.mean(x * x, axis=-1, keepdims=True)
    return x * lax.rsqrt(ms + EPS) * (gain * (1.0 + scale)) + shift


def _params(sem):
    return pltpu.CompilerParams(dimension_semantics=sem, vmem_limit_bytes=VMEM_LIMIT)


def _resident(shape, index_map):
    return pl.BlockSpec(shape, index_map, pipeline_mode=pl.Buffered(1))


ADALN_TN = 1024


def _adaln_kernel(c_ref, w_ref, b_ref, o_ref):
    s = jax.nn.silu(c_ref[...]).astype(BF16)
    o_ref[...] = _dot(s, w_ref[...].astype(BF16)) + b_ref[...]


ADALN_TILES = N_MOD * D_MODEL // ADALN_TN
PREP_STEPS = DEPTH * ADALN_TILES
PREP_PARTS = DEPTH * NA_HEADS


def _prep_kernel(c_ref, aw_ref, ab_ref, rexp_ref, *refs, tables, n_cast):
    mods_ref, bias_ref = refs[n_cast:n_cast + 2]
    _adaln_kernel(c_ref, aw_ref, ab_ref, mods_ref)
    _bias_kernel(rexp_ref, bias_ref, tables=tables)
    for src, dst in zip(refs[:n_cast], refs[n_cast + 2:]):
        dst[...] = src[...].astype(BF16)


def _cast_specs(parts, n_steps, step_of):
    arrays, in_specs, out_shapes, out_specs = [], [], [], []
    for mat, first, rows in parts:
        height = rows // n_steps
        base = first // height
        arrays.append(mat)
        in_specs.append(pl.BlockSpec((height, mat.shape[1]),
                                     lambda *idx, base=base: (base + step_of(*idx), 0)))
        out_shapes.append(jax.ShapeDtypeStruct((rows, mat.shape[1]), BF16))
        out_specs.append(pl.BlockSpec((height, mat.shape[1]), lambda *idx: (step_of(*idx), 0)))
    return arrays, in_specs, out_shapes, out_specs


def _prepare(cvec, ada_w, ada_b, rexp, tables, width, cast_parts):
    rows = cvec.shape[0]
    part = lambda s: jnp.minimum(s, PREP_PARTS - 1)
    tile = lambda s: (s // ADALN_TILES, 0, s % ADALN_TILES)
    head = lambda s: (part(s) // NA_HEADS, part(s) % NA_HEADS, 0, 0)
    mats, cast_in, cast_shapes, cast_out = _cast_specs(cast_parts, PREP_PARTS, part)
    outs = pl.pallas_call(
        functools.partial(_prep_kernel, tables=tables, n_cast=len(mats)),
        out_shape=[jax.ShapeDtypeStruct((DEPTH, rows, N_MOD * D_MODEL), F32),
                   jax.ShapeDtypeStruct((DEPTH, NA_HEADS, width, NA_QROWS * GRID_W), BF16)]
        + cast_shapes,
        grid=(PREP_STEPS,),
        in_specs=[
            pl.BlockSpec((rows, D_MODEL), lambda s: (0, 0)),
            pl.BlockSpec((None, D_MODEL, ADALN_TN), tile),
            pl.BlockSpec((None, 1, ADALN_TN), tile),
            pl.BlockSpec((None, None, N_RPB_COLS + 1, 16 * GRID_W), head),
        ] + cast_in,
        out_specs=[
            pl.BlockSpec((None, rows, ADALN_TN), tile),
            pl.BlockSpec((None, None, width, NA_QROWS * GRID_W), head),
        ] + cast_out,
        compiler_params=_params(("arbitrary",)),
        name="prepare",
    )(cvec, ada_w, ada_b.reshape(DEPTH, 1, N_MOD * D_MODEL), rexp, *mats)
    return outs[0].reshape(DEPTH, rows, N_MOD, D_MODEL), outs[1], outs[2:]


def _ffn_tile(x, gain, shift, scale, gate, wg_ref, wu_ref, wd_ref, side_jobs=()):
    h = _rms_mod(x, gain, shift, scale).astype(BF16)
    acc = jnp.zeros(x.shape, F32)
    n_chunks = D_FF // FFN_FC
    per_chunk = -(-len(side_jobs) // n_chunks)
    for c in range(n_chunks):
        finishers = [job() for job in side_jobs[c * per_chunk:(c + 1) * per_chunk]]
        lo = c * FFN_FC
        g = _dot(h, wg_ref[:, lo:lo + FFN_FC])
        u = _dot(h, wu_ref[:, lo:lo + FFN_FC])
        a = (g * jax.nn.sigmoid(g) * u).astype(BF16)
        acc = acc + _dot(a, wd_ref[lo:lo + FFN_FC, :])
        for finish in finishers:
            finish()
    return x + 0.5 * gate * acc


def _ffn_kernel(x_ref, mod_ref, gain_ref, wg_ref, wu_ref, wd_ref, o_ref, *, mod0, gain_row,
                side_jobs=()):
    m = mod_ref[...]
    o_ref[...] = _ffn_tile(x_ref[...], gain_ref[gain_row:gain_row + 1, :],
                           m[mod0:mod0 + 1], m[mod0 + 1:mod0 + 2], m[mod0 + 2:mod0 + 3],
                           wg_ref, wu_ref, wd_ref, side_jobs)


def _mod_spec(layer, row0, per_group, group_axis=0):
    return pl.BlockSpec((None, None, N_MOD, D_MODEL),
                        lambda *idx: (layer, row0 + idx[group_axis] * per_group, 0, 0))


def _whole(array):
    return _resident(array.shape, lambda *_: (0, 0))


N_TILES = GROUP // FFN_TM
N_STEPS = N_TILES * N_GROUPS
SIDE_PAIRS = HEAD_PAIRS // N_TILES


def _tile_spec(width):
    return pl.BlockSpec((None, FFN_TM, width), lambda t, g: (g, t, 0))


def _step(t, g):
    return t * N_GROUPS + g


def _one_stage(fn):
    def start():
        fn()
        return lambda: None
    return start


def _joint_kernel(*refs, main, n_main, sides):
    outs = refs[n_main + sum(n_in for _, n_in, _ in sides):]
    jobs = []
    i, o = n_main, 1
    for jobs_fn, n_in, n_out in sides:
        jobs += jobs_fn(*refs[i:i + n_in], *outs[o:o + n_out])
        i, o = i + n_in, o + n_out
    main(*refs[:n_main], outs[0], side_jobs=jobs)


def _tile_call(name, main, main_args, main_specs, x_shape, sides=()):
    kern = functools.partial(
        _joint_kernel, main=main, n_main=len(main_args),
        sides=tuple((fn, len(args), len(shapes)) for fn, args, _, shapes, _ in sides))
    outs = pl.pallas_call(
        kern,
        out_shape=[jax.ShapeDtypeStruct(x_shape, F32)] + [s for side in sides for s in side[3]],
        grid=(N_TILES, N_GROUPS),
        in_specs=list(main_specs) + [s for side in sides for s in side[2]],
        out_specs=[_tile_spec(D_MODEL)] + [s for side in sides for s in side[4]],
        compiler_params=_params(("arbitrary", "arbitrary")),
        name=name,
    )(*main_args, *[a for side in sides for a in side[1]])
    return outs


def _ffn(x, mods, norm_g, wg, wu, wd, *, layer, half, row0, per_group, sides=()):
    kern = functools.partial(_ffn_kernel, mod0=6 * half, gain_row=2 * half)
    specs = [
        _tile_spec(D_MODEL),
        _mod_spec(layer, row0, per_group, group_axis=1),
        _resident((None, 3, D_MODEL), lambda t, g: (layer, 0, 0)),
        _whole(wg), _whole(wu), _whole(wd),
    ]
    return _tile_call("ffn", kern, (x, mods, norm_g, wg, wu, wd), specs, x.shape, sides)


def _to_bf16_side(cast_parts):
    mats, in_specs, out_shapes, out_specs = _cast_specs(cast_parts, N_STEPS, _step)

    def jobs(*refs):
        def cast(src, dst):
            def run():
                dst[...] = src[...].astype(BF16)
            return _one_stage(run)
        return [cast(src, dst) for src, dst in zip(refs[:len(mats)], refs[len(mats):])]

    return jobs, mats, in_specs, out_shapes, out_specs


LANES = 128
GROUP_COLS = 256


def _pool_column(a, lane0, seq_len):
    t = a.shape[0]
    pos = lax.broadcasted_iota(jnp.int32, a.shape, 0) & (seq_len - 1)
    lane = lax.broadcasted_iota(jnp.int32, (1, LANES), 1) + lane0
    half = jnp.left_shift(1, lane >> 6)
    max_half = 1 << ((lane0 + LANES - 1) // HEAD_DIM)
    right = a
    left = a
    k = 1
    while 2 * k <= max_half:
        grow = half >= 2 * k
        right = right + jnp.where(pos < jnp.where(grow, seq_len - k, 0),
                                  pltpu.roll(right, t - k, 0), 0.0)
        left = left + jnp.where(pos >= jnp.where(grow, k, seq_len),
                                pltpu.roll(left, k, 0), 0.0)
        k *= 2
    window = right + jnp.where(pos >= 1, pltpu.roll(left, 1, 0), 0.0)
    cnt = jnp.minimum(pos + half, seq_len) - jnp.maximum(pos - half, 0)
    return window / cnt.astype(F32) - a


def _pool_mixer(a, seq_len):
    cols = [_pool_column(a[:, c:c + LANES], c, seq_len) for c in range(0, a.shape[1], LANES)]
    return jnp.concatenate(cols, axis=-1)


def _group_rms(z, ones_ref):
    outs = []
    for c in range(z.shape[1] // GROUP_COLS):
        zc = z[:, c * GROUP_COLS:(c + 1) * GROUP_COLS]
        ms = _dot((zc * zc).astype(BF16), ones_ref[...]) * (1.0 / HEAD_DIM)
        outs.append(zc * lax.rsqrt(ms + EPS))
    return outs[0] if len(outs) == 1 else jnp.concatenate(outs, axis=-1)


def _mixer_in_kernel(x_ref, mod_ref, gain_ref, w_in_ref, pool_w_ref, pool_s_ref,
                     qg_ref, kg_ref, vng_ref, sgw_ref, sgb_ref, og_ref, ones_ref,
                     *refs, seq_len, n_alias):
    oa_ref, q_ref, k_ref, vt_ref, oc_ref, *kv_refs = refs[n_alias:]
    m = mod_ref[...]
    og = og_ref[...]
    gc = og[:, POOL_WIDTH + NA_WIDTH:]
    lane_grp = lax.broadcasted_iota(jnp.int32, (CHUNK, SG_WIDTH), 1) >> 6
    sgw = sgw_ref[...]
    sgb = sgb_ref[...]
    splits = (0, POOL_WIDTH, POOL_WIDTH + NA_WIDTH, POOL_WIDTH + 2 * NA_WIDTH,
              POOL_WIDTH + 3 * NA_WIDTH, POOL_WIDTH + 3 * NA_WIDTH + SG_WIDTH, w_in_ref.shape[1])

    def pool_out(rows, za):
        pooled = _pool_mixer(za, seq_len)
        a_out = _dot(pooled.astype(BF16), pool_w_ref[...]) * pool_s_ref[...]
        oa_ref[rows, :] = _rms(a_out, og[:, 0:POOL_WIDTH]).astype(BF16)

    def cache_out(ref, rows, val_t):
        for s in range((rows.stop - rows.start) // SEQ):
            for hd in range(NA_HEADS):
                ref[rows.start // SEQ + s, hd] = val_t[hd * HEAD_DIM:(hd + 1) * HEAD_DIM,
                                                       s * SEQ:(s + 1) * SEQ]

    def branches(rows, za, zq, zk, zv, zu, zsv):
        def chunk_mixer():
            u = jax.nn.gelu(zu)
            vg = (_group_rms(jax.nn.gelu(zsv), ones_ref) * vng_ref[...]).astype(BF16)
            for r0 in range(0, rows.stop - rows.start, CHUNK):
                full = _dot(sgw, vg[r0:r0 + CHUNK, :])
                sp = full[0:CHUNK]
                for gi in range(1, 4):
                    sp = jnp.where(lane_grp == gi, full[gi * CHUNK:(gi + 1) * CHUNK], sp)
                c_out = u[r0:r0 + CHUNK, :] * (sp + sgb)
                oc_ref[rows.start + r0:rows.start + r0 + CHUNK, :] = (
                    _rms(c_out, gc).astype(BF16))

        def pooling():
            if seq_len <= min(MIX_TILES):
                pool_out(rows, za)

        def queries():
            q = _group_rms(zq, ones_ref) * (qg_ref[...] * QUERY_SCALE)
            q_ref[rows, :] = q.astype(BF16)

        def keys():
            k = _group_rms(zk, ones_ref) * kg_ref[...]
            k_ref[rows, :] = k.astype(BF16)
            if kv_refs:
                cache_out(kv_refs[0], rows, k.T)

        def values():
            v_t = zv.T
            vt_ref[:, rows] = v_t.astype(BF16)
            if kv_refs:
                cache_out(kv_refs[1], rows, v_t)

        return [chunk_mixer, pooling, queries, keys, values]

    starts = [sum(MIX_TILES[:i]) for i in range(len(MIX_TILES))]
    tiles = [slice(r, r + n) for r, n in zip(starts, MIX_TILES)]
    order = (5, 4, 0, 1, 2, 3)
    pending, pooled_inputs = [], []
    for rows in tiles:
        h = _rms_mod(x_ref[rows, :], gain_ref[1:2, :], m[3:4], m[4:5]).astype(BF16)
        z = {}
        for n, i in enumerate(order):
            z[i] = _dot(h, w_in_ref[:, splits[i]:splits[i + 1]])
            if n < len(pending):
                pending[n]()
        pending = branches(rows, *[z[i] for i in range(6)])
        pooled_inputs.append(z[0])
    for branch in pending:
        branch()

    if seq_len > min(MIX_TILES):
        pool_out(slice(0, GROUP), jnp.concatenate(pooled_inputs, axis=0))


def _mixer_in(x, mods, norm_g, w_in, pool_w, pool_s, qg, kg, vng, sgw, sgb, og, ones,
              *, layer, row0, per_group, seq_len, emit_kv, prev_kv=()):
    kern = functools.partial(_mixer_in_kernel, seq_len=seq_len, n_alias=len(prev_kv))
    tok = lambda w, dt: jax.ShapeDtypeStruct((N_GROUPS, GROUP, w), dt)
    tok_spec = lambda w: pl.BlockSpec((None, GROUP, w), lambda g: (g, 0, 0))
    lay = lambda *shape: _resident((None,) + shape, lambda g: (layer,) + (0,) * len(shape))
    n_seq = GROUP // SEQ
    kv_shape = jax.ShapeDtypeStruct((N_GROUPS * n_seq, DEPTH, NA_HEADS, HEAD_DIM, SEQ), F32)
    kv_spec = pl.BlockSpec((n_seq, None, NA_HEADS, HEAD_DIM, SEQ), lambda g: (g, layer, 0, 0, 0))
    kv_shapes = (kv_shape, kv_shape) if emit_kv else ()
    kv_specs = (kv_spec, kv_spec) if emit_kv else ()
    n_in = 13
    aliases = {n_in + i: 5 + i for i in range(len(prev_kv))}
    return pl.pallas_call(
        kern,
        out_shape=(tok(POOL_WIDTH, BF16), tok(NA_WIDTH, BF16), tok(NA_WIDTH, BF16),
                   jax.ShapeDtypeStruct((N_GROUPS, NA_WIDTH, GROUP), BF16),
                   tok(SG_WIDTH, BF16)) + kv_shapes,
        grid=(N_GROUPS,),
        in_specs=[
            tok_spec(D_MODEL),
            _mod_spec(layer, row0, per_group),
            lay(3, D_MODEL),
            _resident(w_in.shape, lambda g: (0, 0)),
            lay(POOL_WIDTH, POOL_WIDTH),
            lay(1, POOL_WIDTH),
            lay(1, NA_WIDTH),
            lay(1, NA_WIDTH),
            lay(1, SG_WIDTH),
            lay(4 * CHUNK, CHUNK),
            lay(CHUNK, SG_WIDTH),
            lay(1, D_MODEL),
            _resident((GROUP_COLS, GROUP_COLS), lambda g: (0, 0)),
        ] + [pl.BlockSpec(memory_space=pl.ANY)] * len(prev_kv),
        out_specs=(tok_spec(POOL_WIDTH), tok_spec(NA_WIDTH), tok_spec(NA_WIDTH),
                   pl.BlockSpec((None, NA_WIDTH, GROUP), lambda g: (g, 0, 0)),
                   tok_spec(SG_WIDTH)) + kv_specs,
        input_output_aliases=aliases,
        compiler_params=_params(("arbitrary",)),
        name="mixer_in",
    )(x, mods, norm_g, w_in, pool_w, pool_s, qg, kg, vng, sgw, sgb, og, ones, *prev_kv)


def _first_head(shape, axis):
    return lax.broadcasted_iota(jnp.int32, shape, axis) < HEAD_DIM


def _head_queries(q):
    first = _first_head(q.shape, 1)
    zero = jnp.zeros_like(q)
    return jnp.where(first, q, zero), jnp.where(first, zero, q)


def _head_values_t(vt):
    first = _first_head(vt.shape, 0)
    one = jnp.ones_like(vt)
    return jnp.where(first, vt, one), jnp.where(first, one, vt)


def _softmax_pv_t(scores_t, values_t):
    mx = scores_t[0].max(axis=0, keepdims=True)
    for s in scores_t[1:]:
        mx = jnp.maximum(mx, s.max(axis=0, keepdims=True))
    out = 0.0
    for s, vt in zip(scores_t, values_t):
        out = out + _dot(vt, jnp.exp2(s - mx).astype(BF16))
    return out


def _merge_heads_t(o0, o1):
    first = _first_head(o0.shape, 0)
    num = jnp.where(first, o0, o1)
    den = pltpu.roll(jnp.where(first, o1, o0), HEAD_DIM, 0)
    return (num / den).T.astype(BF16)


def _ctx_attn_jobs(q_ref, k_ref, vt_ref, o_ref):
    def job(rows, lanes):
        def start():
            k = k_ref[rows, lanes]
            q0, q1 = _head_queries(q_ref[rows, lanes])
            s0, s1 = _dot_nt(k, q0), _dot_nt(k, q1)

            def finish():
                v0, v1 = _head_values_t(vt_ref[lanes, rows])
                o_ref[rows, lanes] = _merge_heads_t(_softmax_pv_t([s0], [v0]),
                                                    _softmax_pv_t([s1], [v1]))
            return finish
        return start

    return [job(slice(s, s + SEQ), slice(p, p + PAIR_W))
            for p in range(0, q_ref.shape[1], PAIR_W) for s in range(0, GROUP, SEQ)]


def _side_spec():
    return pl.BlockSpec((None, GROUP, SIDE_PAIRS * PAIR_W), lambda t, g: (g, 0, t))


def _side_spec_t():
    return pl.BlockSpec((None, SIDE_PAIRS * PAIR_W, GROUP), lambda t, g: (g, t, 0))


def _ctx_attn_side(q, k, vt):
    spec = _side_spec()
    return (_ctx_attn_jobs, (q, k, vt), [spec, spec, _side_spec_t()],
            [jax.ShapeDtypeStruct(q.shape, BF16)], [spec])


NA_QROWS = 4


def _na_plan():
    blocks, tables, offsets, width = [], [], [], 0
    for r0 in range(0, GRID_ROWS, NA_QROWS):
        rs = range(r0, r0 + NA_QROWS)
        starts = [min(max(r - NA_ROWS // 2, 0), GRID_ROWS - NA_ROWS) for r in rs]
        nk = max(starts) + NA_ROWS - min(starts)
        nk += nk % 2
        k0 = min(min(starts), GRID_ROWS - nk)
        table = (nk, tuple((k0 - r + NA_ROWS - 1, s - k0, s - k0 + NA_ROWS)
                           for r, s in zip(rs, starts)))
        assert all(d >= 0 and d + nk <= N_RPB_ROWS + 1 for d, _, _ in table[1])
        if table not in tables:
            tables.append(table)
            offsets.append(width)
            width += nk * GRID_W
        blocks.append((r0, k0, nk, offsets[tables.index(table)]))
    return tuple(blocks), tuple(tables), width


def _bias_kernel(rexp_ref, o_ref, *, tables):
    shape = (GRID_W, 16 * GRID_W)
    cq = lax.broadcasted_iota(jnp.int32, shape, 0)
    ck = lax.broadcasted_iota(jnp.int32, shape, 1) & (GRID_W - 1)
    dc = jnp.clip(ck - cq, -(NA_COLS - 1), NA_COLS - 1) + (NA_COLS - 1)
    col_start = jnp.clip(cq - NA_COLS // 2, 0, GRID_W - NA_COLS)
    ok = (ck >= col_start) & (ck < col_start + NA_COLS)
    table = jnp.zeros(shape, F32)
    for j in range(N_RPB_COLS):
        table = jnp.where(dc == j, jnp.broadcast_to(rexp_ref[j:j + 1, :], shape), table)
    table = jnp.where(ok, table * LOG2E, MASKED)
    col0 = 0
    for nk, rows in tables:
        width = nk * GRID_W
        krow = lax.broadcasted_iota(jnp.int32, (GRID_W, width), 1) >> 6
        slabs = []
        for d, lo, hi in rows:
            slab = table[:, d * GRID_W:d * GRID_W + width]
            if lo > 0 or hi < nk:
                slab = jnp.where((krow >= lo) & (krow < hi), slab, MASKED)
            slabs.append(slab)
        o_ref[col0:col0 + width, :] = jnp.concatenate(slabs, axis=0).T.astype(o_ref.dtype)
        col0 += width


def _na_attn_jobs(q_ref, k_ref, vt_ref, ck_ref, cv_ref, bias_ref, o_ref, *, blocks):
    pair_cache = {}

    def cached_pair(ref, p):
        key = (id(ref), p)
        if key not in pair_cache:
            pair_cache[key] = jnp.concatenate([ref[2 * p], ref[2 * p + 1]], axis=0).astype(BF16)
        return pair_cache[key]

    def cached_values(p):
        if ("values", p) not in pair_cache:
            pair_cache["values", p] = _head_values_t(cached_pair(cv_ref, p))
        return pair_cache["values", p]

    def job(p, r0, k0, nk, col0):
        lanes = slice(p * PAIR_W, (p + 1) * PAIR_W)
        h0 = 2 * p
        q_rows = slice(r0 * GRID_W, (r0 + NA_QROWS) * GRID_W)
        k_rows = slice(k0 * GRID_W, (k0 + nk) * GRID_W)

        def start():
            kc_t = cached_pair(ck_ref, p)
            k_loc = k_ref[k_rows, lanes]
            scores = []
            for e, qe in enumerate(_head_queries(q_ref[q_rows, lanes])):
                bias = bias_ref[h0 + e, col0:col0 + nk * GRID_W, :].astype(F32)
                s_loc = _dot_nt(k_loc, qe) + bias
                scores.append([s_loc, _dot(qe, kc_t).T])

            def finish():
                vc_t = cached_values(p)
                v_loc = _head_values_t(vt_ref[lanes, k_rows])
                o_ref[q_rows, lanes] = _merge_heads_t(
                    *[_softmax_pv_t(scores[e], [v_loc[e], vc_t[e]]) for e in range(2)])
            return finish
        return start

    return [job(p, *blk) for p in range(q_ref.shape[1] // PAIR_W) for blk in blocks]


def _na_attn_side(q, k, vt, cache_kt, cache_vt, bias, blocks, *, layer):
    spec = _side_spec()
    n_heads = 2 * SIDE_PAIRS
    cspec = pl.BlockSpec((None, None, n_heads, HEAD_DIM, PAST_LEN),
                         lambda t, g: (g, layer, t, 0, 0))
    bspec = pl.BlockSpec((None, n_heads) + bias.shape[2:], lambda t, g: (layer, t, 0, 0))
    return (functools.partial(_na_attn_jobs, blocks=blocks),
            (q, k, vt, cache_kt, cache_vt, bias),
            [spec, spec, _side_spec_t(), cspec, cspec, bspec],
            [jax.ShapeDtypeStruct(q.shape, BF16)], [spec])


def _mixer_out_ffn_kernel(x_ref, oa_ref, ob_ref, oc_ref, mod_ref, gain_ref, og_ref, wo_ref,
                          wg_ref, wu_ref, wd_ref, o_ref, *, side_jobs=()):
    m = mod_ref[...]
    gb = og_ref[:, POOL_WIDTH:POOL_WIDTH + NA_WIDTH]
    c0 = POOL_WIDTH
    c1 = POOL_WIDTH + NA_WIDTH
    o = _dot(oa_ref[...], wo_ref[0:c0, :]) + _dot(oc_ref[...], wo_ref[c1:, :])
    ob = _rms(ob_ref[...].astype(F32), gb).astype(BF16)
    o = o + _dot(ob, wo_ref[c0:c1, :])
    x = x_ref[...] + m[5:6] * o
    o_ref[...] = _ffn_tile(x, gain_ref[2:3, :], m[6:7], m[7:8], m[8:9], wg_ref, wu_ref, wd_ref,
                           side_jobs)


def _mixer_out_ffn(x, oa, ob, oc, mods, norm_g, og, wo, wg, wu, wd, *, layer, row0, per_group,
                   sides=()):
    specs = [
        _tile_spec(D_MODEL), _tile_spec(POOL_WIDTH), _tile_spec(NA_WIDTH), _tile_spec(SG_WIDTH),
        _mod_spec(layer, row0, per_group, group_axis=1),
        _resident((None, 3, D_MODEL), lambda t, g: (layer, 0, 0)),
        _resident((None, 1, D_MODEL), lambda t, g: (layer, 0, 0)),
        _whole(wo), _whole(wg), _whole(wu), _whole(wd),
    ]
    return _tile_call("mixer_out_ffn", _mixer_out_ffn_kernel,
                      (x, oa, ob, oc, mods, norm_g, og, wo, wg, wu, wd), specs, x.shape, sides)


def _block_diag(w):
    depth, groups, d, _ = w.shape
    same = jnp.arange(groups)[:, None] == jnp.arange(groups)[None, :]
    blocks = jnp.where(same[None, :, None, :, None], w[:, :, :, None, :], 0.0)
    return blocks.reshape(depth, groups * d, groups * d)


def kernel(x_prompt, x_sample, cache_k, cache_v, c, c_ctx, ada_w, ada_b, norm_g, ffn_w_gate,
           ffn_w_up, ffn_w_down, w_in, pool_w, pool_scale, q_norm_g, k_norm_g, na_rpb,
           sg_vnorm_g, sg_w, sg_b, out_norm_g, w_out):
    cvec = jnp.concatenate([c_ctx[None], c])

    big = (ffn_w_gate, ffn_w_up, ffn_w_down, w_in, w_out)
    pool_wb = _block_diag(pool_w).astype(BF16)
    pool_s = pool_scale.reshape(DEPTH, 1, POOL_WIDTH)
    qg = jnp.tile(q_norm_g, (1, NA_HEADS)).reshape(DEPTH, 1, NA_WIDTH)
    kg = jnp.tile(k_norm_g, (1, NA_HEADS)).reshape(DEPTH, 1, NA_WIDTH)
    vng = sg_vnorm_g.reshape(DEPTH, 1, SG_WIDTH)
    sgw = sg_w.reshape(DEPTH, 4 * CHUNK, CHUNK).astype(BF16)
    sgb = jnp.repeat(jnp.swapaxes(sg_b, 1, 2), HEAD_DIM, axis=-1)
    og = out_norm_g.reshape(DEPTH, 1, D_MODEL)
    lane = jnp.arange(GROUP_COLS) // HEAD_DIM
    ones = (lane[:, None] == lane[None, :]).astype(BF16)
    rexp = jnp.repeat(jnp.swapaxes(na_rpb, 2, 3), GRID_W, axis=-1)
    rexp = jnp.pad(rexp, ((0, 0), (0, 0), (0, 1), (0, GRID_W)))
    na_blocks, na_tables, na_width = _na_plan()
    cache_kt = jnp.swapaxes(cache_k, 3, 4)
    cache_vt = jnp.swapaxes(cache_v, 3, 4)

    xp = x_prompt.reshape(N_GROUPS, GROUP, D_MODEL)
    xs = x_sample.reshape(N_GROUPS, GROUP, D_MODEL)
    ctx = dict(row0=0, per_group=0)
    lat = dict(row0=1, per_group=1)
    new_kv = ()

    gate, up, down, w_in_m, w_out_m = (w.reshape(-1, w.shape[-1]) for w in big)

    def half_parts(layer, half):
        i = 2 * layer + half
        return [(gate, i * D_MODEL, D_MODEL), (up, i * D_MODEL, D_MODEL), (down, i * D_FF, D_FF)]

    def rest_parts(layer):
        return half_parts(layer, 1) + [(w_in_m, layer * D_MODEL, D_MODEL),
                                       (w_out_m, layer * D_MODEL, D_MODEL)]

    mods, bias, first_half = _prepare(cvec, ada_w, ada_b, rexp, na_tables, na_width,
                                      half_parts(0, 0))
    for l in range(DEPTH):
        last = l == DEPTH - 1
        if l == 0:
            xp, *rest = _ffn(xp, mods, norm_g, *first_half, layer=l, half=0, **ctx,
                             sides=[_to_bf16_side(rest_parts(0))])
        else:
            xp, = _ffn(xp, mods, norm_g, *first_half, layer=l, half=0, **ctx)
        second_half, (w_in_b, wo) = rest[:3], rest[3:]
        mix = (mods, norm_g, w_in_b, pool_wb, pool_s, qg, kg, vng, sgw, sgb, og, ones)
        oa_p, q, k, v, oc_p, *new_kv = _mixer_in(xp, *mix, layer=l, seq_len=SEQ, emit_kv=True,
                                                 prev_kv=new_kv, **ctx)
        xs, ob_p = _ffn(xs, mods, norm_g, *first_half, layer=l, half=0, **lat,
                        sides=[_ctx_attn_side(q, k, v)])
        oa_s, q, k, v, oc_s = _mixer_in(xs, *mix, layer=l, seq_len=GROUP, emit_kv=False, **lat)
        xp, ob_s = _mixer_out_ffn(
            xp, oa_p, ob_p, oc_p, mods, norm_g, og, wo, *second_half, layer=l, **ctx,
            sides=[_na_attn_side(q, k, v, cache_kt, cache_vt, bias, na_blocks, layer=l)])
        sides = [] if last else [_to_bf16_side(half_parts(l + 1, 0) + rest_parts(l + 1))]
        xs, *cast = _mixer_out_ffn(xs, oa_s, ob_s, oc_s, mods, norm_g, og, wo, *second_half,
                                   layer=l, **lat, sides=sides)
        if not last:
            first_half, rest = cast[:3], cast[3:]

    return (xp.reshape(x_prompt.shape), xs.reshape(x_sample.shape),
            jnp.swapaxes(new_kv[0], 3, 4), jnp.swapaxes(new_kv[1], 3, 4))
```
